```python
import jax, jax.numpy as jnp
from jax import lax
import numpy as np

D_MODEL = 2048
BATCH = 4
SEQ = 2048
DEPTH = 4
DEC_BATCH = 128
DEC_SEQ = 4
PAST_LEN = 8192
PAGE_SIZE = 128

N_EVEN = (DEPTH + 1) // 2
N_ODD = DEPTH // 2
HEAD_DIM = 128
A_HEADS = 8
A_WIDTH = A_HEADS * HEAD_DIM
CHUNK = 128
B_HEADS = 8
Q_RANK = 768
KV_RANK = 512
DH_NOPE = 128
DH_ROPE = 64
DH_V = 128
B_WIDTH = B_HEADS * DH_V
ROPE_THETA = 10000.0
SM_SCALE = (DH_NOPE + DH_ROPE) ** -0.5
Q_BLOCK = 128
C_WIDTH = 1024
CONV_W = 3
POOL_WINDOWS = (2, 4, 8, 16)
D_WIDTH = 1024
D_GROUP = D_WIDTH // len(POOL_WINDOWS)
MAX_WIN = max(POOL_WINDOWS)
IN_EVEN = 2 * A_WIDTH + Q_RANK + KV_RANK + DH_ROPE
IN_ODD = 3 * C_WIDTH + D_WIDTH
MIX_EVEN = A_WIDTH + B_WIDTH
MIX_ODD = C_WIDTH + D_WIDTH
N_GROUPS = 4
EXPERTS_PER_GROUP = 8
N_EXPERTS = N_GROUPS * EXPERTS_PER_GROUP
TOP_K = 2
D_EXPERT = 512
MOE_BLOCK = 128
ALPHA = (2 * DEPTH) ** 0.25
BETA = (8 * DEPTH) ** -0.25
LN_EPS = 1e-5
RMS_EPS = 1e-6

kernel_name = 'hybrid_sgu_mla_conv_pool_hmoe_step'


def _layer_norm(x, g, b):
    xf = x.astype(jnp.float32)
    mu = jnp.mean(xf, -1, keepdims=True)
    var = jnp.mean(jnp.square(xf - mu), -1, keepdims=True)
    return ((xf - mu) * lax.rsqrt(var + LN_EPS) * g + b).astype(x.dtype)


def _rms_norm(x, g):
    xf = x.astype(jnp.float32)
    return (xf * lax.rsqrt(jnp.mean(jnp.square(xf), -1, keepdims=True) + RMS_EPS) * g).astype(x.dtype)


def _rope_tables(pos):
    inv = ROPE_THETA ** (-jnp.arange(0, DH_ROPE, 2, dtype=jnp.float32) / DH_ROPE)
    ang = pos[:, None] * inv[None, :]
    return jnp.cos(ang), jnp.sin(ang)


def _rope(x, cos, sin):
    xf = x.astype(jnp.float32)
    x1, x2 = xf[..., :DH_ROPE // 2], xf[..., DH_ROPE // 2:]
    return jnp.concatenate([x1 * cos - x2 * sin, x1 * sin + x2 * cos], -1).astype(x.dtype)


def _even_project(x, pos, w_in, sgu_ln_g, sgu_ln_b, q_norm_g, w_q_up, kv_norm_g, w_uk):
    h = x @ w_in
    s1 = A_WIDTH
    s2 = 2 * A_WIDTH
    s3 = s2 + Q_RANK
    s4 = s3 + KV_RANK
    a_u, a_v, q_lat, kv_lat, k_r = jnp.split(h, [s1, s2, s3, s4], axis=-1)
    u = jax.nn.gelu(a_u)
    vn = _layer_norm(jax.nn.gelu(a_v), sgu_ln_g, sgu_ln_b)
    cos, sin = _rope_tables(pos)
    q = jnp.einsum('btr,rhd->bthd', _rms_norm(q_lat, q_norm_g), w_q_up)
    q_pe = _rope(q[..., DH_NOPE:], cos[:, None], sin[:, None])
    q_abs = jnp.einsum('bthd,chd->bthc', q[..., :DH_NOPE], w_uk)
    ckv = _rms_norm(kv_lat, kv_norm_g)
    kpe = _rope(k_r, cos, sin)
    return u, vn, q_abs, q_pe, ckv, kpe


def _sgu(u, vn, sgu_w, sgu_b):
    b, t, _ = vn.shape
    l = min(t, CHUNK)
    v = vn.reshape(b, t // l, l, A_HEADS, HEAD_DIM)
    w = jnp.tril(sgu_w[:, :l, :l])
    mixed = jnp.einsum('hij,bcjhd->bcihd', w, v) + sgu_b[:, :l].T[:, :, None]
    return u * mixed.reshape(b, t, A_WIDTH)


def _latent_scores(q_abs, q_pe, kv, kpe):
    s = jnp.einsum('bqhc,bkc->bhqk', q_abs, kv) + jnp.einsum('bqhr,bkr->bhqk', q_pe, kpe)
    return s.astype(jnp.float32) * SM_SCALE


def _mla_prompt(q_abs, q_pe, ckv, kpe):
    b, s, h, c = q_abs.shape
    nb = s // Q_BLOCK
    qa = q_abs.reshape(b, nb, Q_BLOCK, h, c).swapaxes(0, 1)
    qp = q_pe.reshape(b, nb, Q_BLOCK, h, DH_ROPE).swapaxes(0, 1)
    key_pos = jnp.arange(s)

    def block(args):
        qa_b, qp_b, start = args
        q_pos = start + jnp.arange(Q_BLOCK)
        sc = _latent_scores(qa_b, qp_b, ckv, kpe)
        sc = jnp.where(key_pos[None, :] <= q_pos[:, None], sc, -jnp.inf)
        p = jax.nn.softmax(sc, axis=-1).astype(ckv.dtype)
        return jnp.einsum('bhqk,bkc->bqhc', p, ckv)

    o = lax.map(block, (qa, qp, jnp.arange(nb) * Q_BLOCK))
    return o.swapaxes(0, 1).reshape(b, s, h, c)


def _mla_sample(q_abs, q_pe, ckv, kpe, past_kv, past_pe):
    t = q_abs.shape[1]
    p_len = past_kv.shape[1]
    s_past = _latent_scores(q_abs, q_pe, past_kv, past_pe)
    s_new = _latent_scores(q_abs, q_pe, ckv, kpe)
    s_new = jnp.where(jnp.tril(jnp.ones((t, t), bool)), s_new, -jnp.inf)
    p = jax.nn.softmax(jnp.concatenate([s_past, s_new], -1), axis=-1).astype(ckv.dtype)
    return (jnp.einsum('bhqk,bkc->bqhc', p[..., :p_len], past_kv)
            + jnp.einsum('bhqk,bkc->bqhc', p[..., p_len:], ckv))


def _even_prompt(x, w_in, sgu_ln_g, sgu_ln_b, sgu_w, sgu_b, q_norm_g, w_q_up, kv_norm_g, w_uk, w_uv, w_out):
    b, s, _ = x.shape
    pos = jnp.arange(s, dtype=jnp.float32)
    u, vn, q_abs, q_pe, ckv, kpe = _even_project(x, pos, w_in, sgu_ln_g, sgu_ln_b, q_norm_g, w_q_up, kv_norm_g, w_uk)
    y_a = _sgu(u, vn, sgu_w, sgu_b)
    o = jnp.einsum('bthc,chd->bthd', _mla_prompt(q_abs, q_pe, ckv, kpe), w_uv).reshape(b, s, B_WIDTH)
    y = jnp.concatenate([y_a, o], -1) @ w_out
    return y, ckv, kpe


def _even_sample(x, past_kv, past_pe, w_in, sgu_ln_g, sgu_ln_b, sgu_w, sgu_b, q_norm_g, w_q_up, kv_norm_g, w_uk, w_uv, w_out):
    b, t, _ = x.shape
    pos = past_kv.shape[1] + jnp.arange(t, dtype=jnp.float32)
    u, vn, q_abs, q_pe, ckv, kpe = _even_project(x, pos, w_in, sgu_ln_g, sgu_ln_b, q_norm_g, w_q_up, kv_norm_g, w_uk)
    y_a = _sgu(u, vn, sgu_w, sgu_b)
    o = jnp.einsum('bthc,chd->bthd', _mla_sample(q_abs, q_pe, ckv, kpe, past_kv, past_pe), w_uv).reshape(b, t, B_WIDTH)
    y = jnp.concatenate([y_a, o], -1) @ w_out
    return y, ckv, kpe, vn


def _odd_mixer(x, conv_prefix, pool_prefix, n_valid, w_in, conv_w, pool_w, pool_scale, w_out):
    t = x.shape[1]
    h = x @ w_in
    c_b, c_c, c_x, d_in = jnp.split(h, [C_WIDTH, 2 * C_WIDTH, 3 * C_WIDTH], axis=-1)
    zp = jnp.concatenate([conv_prefix, c_c * c_x], axis=1)
    conv = zp[:, 0:t] * conv_w[0]
    for k in range(1, CONV_W):
        conv = conv + zp[:, k:k + t] * conv_w[k]
    out_c = c_b * conv
    pp = jnp.concatenate([pool_prefix, d_in], axis=1)
    cs = jnp.cumsum(pp.astype(jnp.float32), axis=1)
    cs = jnp.concatenate([jnp.zeros_like(cs[:, :1]), cs], axis=1)
    t_idx = jnp.arange(t)
    outs = []
    for g, w in enumerate(POOL_WINDOWS):
        sl = slice(g * D_GROUP, (g + 1) * D_GROUP)
        win_sum = cs[:, MAX_WIN:MAX_WIN + t, sl] - cs[:, MAX_WIN - w:MAX_WIN - w + t, sl]
        count = jnp.minimum(n_valid + t_idx + 1, w).astype(jnp.float32)
        diff = win_sum / count[None, :, None] - d_in[..., sl].astype(jnp.float32)
        outs.append(jnp.einsum('btc,cd->btd', diff.astype(x.dtype), pool_w[g]))
    out_d = jnp.concatenate(outs, -1) * pool_scale
    y = jnp.concatenate([out_c, out_d], -1) @ w_out
    return y, zp[:, -(CONV_W - 1):], pp[:, -(MAX_WIN - 1):]


def _expert_ffn(xf, experts, gates, w_gate, w_up, w_down):
    n, d = xf.shape
    a = n * TOP_K
    flat_e = experts.reshape(-1).astype(jnp.int32)
    flat_tok = jnp.repeat(jnp.arange(n, dtype=jnp.int32), TOP_K)
    flat_g = gates.reshape(-1).astype(jnp.float32)
    order = jnp.argsort(flat_e)
    se, stok, sg = flat_e[order], flat_tok[order], flat_g[order]
    counts = jnp.zeros((N_EXPERTS,), jnp.int32).at[flat_e].add(1)
    padded = (counts + MOE_BLOCK - 1) // MOE_BLOCK * MOE_BLOCK
    pad_end = jnp.cumsum(padded)
    pad_start = pad_end - padded
    start = jnp.cumsum(counts) - counts
    dest = pad_start[se] + jnp.arange(a, dtype=jnp.int32) - start[se]
    n_blocks = -(-a // MOE_BLOCK) + N_EXPERTS
    rows = n_blocks * MOE_BLOCK
    row_tok = jnp.zeros((rows,), jnp.int32).at[dest].set(stok)
    row_gate = jnp.zeros((rows,), jnp.float32).at[dest].set(sg)
    block_e = jnp.minimum(jnp.searchsorted(pad_end, jnp.arange(n_blocks) * MOE_BLOCK, side='right'), N_EXPERTS - 1)
    xb = xf[row_tok].reshape(n_blocks, MOE_BLOCK, d)

    def run(args):
        xblk, e = args
        return (jax.nn.silu(xblk @ w_gate[e]) * (xblk @ w_up[e])) @ w_down[e]

    yb = lax.map(run, (xb, block_e)).reshape(rows, d)
    y = jnp.zeros((n, d), jnp.float32).at[row_tok].add(yb.astype(jnp.float32) * row_gate[:, None])
    return y.astype(xf.dtype)


def _moe(x, rc_w, rc_b, rf_w, rf_b, w_gate, w_up, w_down):
    b, t, d = x.shape
    xf = x.reshape(b * t, d)
    x32 = xf.astype(jnp.float32)
    pc = jax.nn.softmax(x32 @ rc_w.astype(jnp.float32) + rc_b, axis=-1)
    g_sel = jnp.argmax(pc, axis=-1)
    p_g = jnp.take_along_axis(pc, g_sel[:, None], axis=-1)
    lf = (x32 @ rf_w.astype(jnp.float32) + rf_b).reshape(-1, N_GROUPS, EXPERTS_PER_GROUP)
    lf = jnp.take_along_axis(lf, g_sel[:, None, None], axis=1)[:, 0]
    top_p, top_j = lax.top_k(jax.nn.softmax(lf, axis=-1), TOP_K)
    gates = p_g * top_p / jnp.sum(top_p, -1, keepdims=True)
    experts = g_sel[:, None] * EXPERTS_PER_GROUP + top_j
    return _expert_ffn(xf, experts, gates, w_gate, w_up, w_down).reshape(b, t, d)


def setup_inputs(seed: int = 0) -> dict:
    key = jax.random.key(seed)
    ks = iter(jax.random.split(key, 40))
    n_pages = PAST_LEN // PAGE_SIZE
    n_used = DEC_BATCH * n_pages
    pool_pages = n_used + max(1, n_used // 4)
    f32 = jnp.float32

    def nrm(shape, scale):
        return jax.random.normal(next(ks), shape, f32) * scale

    def gain(shape):
        return 1.0 + 0.05 * jax.random.normal(next(ks), shape, f32)

    page_table = jax.random.permutation(next(ks), pool_pages)[:n_used].reshape(DEC_BATCH, n_pages).astype(jnp.int32)
    return {
        'x_prompt': nrm((BATCH, SEQ, D_MODEL), 1.0),
        'x_sample': nrm((DEC_BATCH, DEC_SEQ, D_MODEL), 1.0),
        'cache_kv_latent': nrm((N_EVEN, pool_pages, PAGE_SIZE, KV_RANK), 1.0),
        'cache_k_rope': nrm((N_EVEN, pool_pages, PAGE_SIZE, DH_ROPE), 1.0),
        'state_conv': nrm((N_ODD, DEC_BATCH, CONV_W - 1, C_WIDTH), 1.0),
        'state_pool': nrm((N_ODD, DEC_BATCH, MAX_WIN - 1, D_WIDTH), 1.0),
        'page_table': page_table,
        'w_in_even': nrm((N_EVEN, D_MODEL, IN_EVEN), D_MODEL ** -0.5),
        'sgu_ln_g': gain((N_EVEN, A_WIDTH)),
        'sgu_ln_b': nrm((N_EVEN, A_WIDTH), 0.02),
        'sgu_w': nrm((N_EVEN, A_HEADS, CHUNK, CHUNK), CHUNK ** -0.5),
        'sgu_b': 1.0 + nrm((N_EVEN, A_HEADS, CHUNK), 0.1),
        'q_norm_g': gain((N_EVEN, Q_RANK)),
        'w_q_up': nrm((N_EVEN, Q_RANK, B_HEADS, DH_NOPE + DH_ROPE), Q_RANK ** -0.5),
        'kv_norm_g': gain((N_EVEN, KV_RANK)),
        'w_uk': nrm((N_EVEN, KV_RANK, B_HEADS, DH_NOPE), KV_RANK ** -0.5),
        'w_uv': nrm((N_EVEN, KV_RANK, B_HEADS, DH_V), KV_RANK ** -0.5),
        'w_out_even': nrm((N_EVEN, MIX_EVEN, D_MODEL), MIX_EVEN ** -0.5 * BETA),
        'w_in_odd': nrm((N_ODD, D_MODEL, IN_ODD), D_MODEL ** -0.5),
        'conv_w': nrm((N_ODD, CONV_W, C_WIDTH), CONV_W ** -0.5),
        'pool_w': nrm((N_ODD, len(POOL_WINDOWS), D_GROUP, D_GROUP), D_GROUP ** -0.5),
        'pool_scale': 1.0 + nrm((N_ODD, D_WIDTH), 0.1),
        'w_out_odd': nrm((N_ODD, MIX_ODD, D_MODEL), MIX_ODD ** -0.5 * BETA),
        'ln_mix_g': gain((DEPTH, D_MODEL)),
        'ln_mix_b': nrm((DEPTH, D_MODEL), 0.02),
        'ln_ffn_g': gain((DEPTH, D_MODEL)),
        'ln_ffn_b': nrm((DEPTH, D_MODEL), 0.02),
        'router_coarse_w': nrm((DEPTH, D_MODEL, N_GROUPS), D_MODEL ** -0.5),
        'router_coarse_b': nrm((DEPTH, N_GROUPS), 0.01),
        'router_fine_w': nrm((DEPTH, D_MODEL, N_EXPERTS), D_MODEL ** -0.5),
        'router_fine_b': nrm((DEPTH, N_EXPERTS), 0.01),
        'w_exp_gate': nrm((DEPTH, N_EXPERTS, D_MODEL, D_EXPERT), D_MODEL ** -0.5),
        'w_exp_up': nrm((DEPTH, N_EXPERTS, D_MODEL, D_EXPERT), D_MODEL ** -0.5),
        'w_exp_down': nrm((DEPTH, N_EXPERTS, D_EXPERT, D_MODEL), D_EXPERT ** -0.5 * BETA),
    }


def reference(x_prompt, x_sample, cache_kv_latent, cache_k_rope, state_conv, state_pool, page_table,
              w_in_even, sgu_ln_g, sgu_ln_b, sgu_w, sgu_b, q_norm_g, w_q_up, kv_norm_g, w_uk, w_uv, w_out_even,
              w_in_odd, conv_w, pool_w, pool_scale, w_out_odd,
              ln_mix_g, ln_mix_b, ln_ffn_g, ln_ffn_b,
              router_coarse_w, router_coarse_b, router_fine_w, router_fine_b,
              w_exp_gate, w_exp_up, w_exp_down):
    xp, xs = x_prompt, x_sample
    bp, bs = xp.shape[0], xs.shape[0]
    past_len = page_table.shape[1] * PAGE_SIZE
    lat_p, pe_p, conv_p, pool_p = [], [], [], []
    lat_s, pe_s, v_s, conv_s, pool_s = [], [], [], [], []
    for l in range(DEPTH):
        i = l // 2
        if l % 2 == 0:
            ew = (w_in_even[i], sgu_ln_g[i], sgu_ln_b[i], sgu_w[i], sgu_b[i], q_norm_g[i], w_q_up[i],
                  kv_norm_g[i], w_uk[i], w_uv[i], w_out_even[i])
            yp, ckv_p, kpe_p = _even_prompt(xp, *ew)
            past_kv = cache_kv_latent[i, page_table].reshape(bs, past_len, KV_RANK)
            past_pe = cache_k_rope[i, page_table].reshape(bs, past_len, DH_ROPE)
            ys, ckv_s, kpe_s, vn_s = _even_sample(xs, past_kv, past_pe, *ew)
            lat_p.append(ckv_p)
            pe_p.append(kpe_p)
            lat_s.append(ckv_s)
            pe_s.append(kpe_s)
            v_s.append(vn_s)
        else:
            ow = (w_in_odd[i], conv_w[i], pool_w[i], pool_scale[i], w_out_odd[i])
            yp, cp, pp = _odd_mixer(xp, jnp.zeros((bp, CONV_W - 1, C_WIDTH), xp.dtype),
                                    jnp.zeros((bp, MAX_WIN - 1, D_WIDTH), xp.dtype), 0, *ow)
            ys, cs, ps = _odd_mixer(xs, state_conv[i], state_pool[i], past_len, *ow)
            conv_p.append(cp)
            pool_p.append(pp)
            conv_s.append(cs)
            pool_s.append(ps)
        xp = _layer_norm(ALPHA * xp + yp, ln_mix_g[l], ln_mix_b[l])
        xs = _layer_norm(ALPHA * xs + ys, ln_mix_g[l], ln_mix_b[l])
        mw = (router_coarse_w[l], router_coarse_b[l], router_fine_w[l], router_fine_b[l],
              w_exp_gate[l], w_exp_up[l], w_exp_down[l])
        xp = _layer_norm(ALPHA * xp + _moe(xp, *mw), ln_ffn_g[l], ln_ffn_b[l])
        xs = _layer_norm(ALPHA * xs + _moe(xs, *mw), ln_ffn_g[l], ln_ffn_b[l])
    return (xp, xs,
            jnp.stack(lat_p), jnp.stack(pe_p), jnp.stack(conv_p), jnp.stack(pool_p),
            jnp.stack(lat_s), jnp.stack(pe_s), jnp.stack(v_s), jnp.stack(conv_s), jnp.stack(pool_s))
```

```python
from functools import partial

import jax
import jax.numpy as jnp
import numpy as np
from jax import lax
from jax.experimental import pallas as pl
from jax.experimental.pallas import tpu as pltpu

D_MODEL = 2048
DEPTH = 4
PAGE_SIZE = 128
HEAD_DIM = 128
A_HEADS = 8
A_WIDTH = A_HEADS * HEAD_DIM
CHUNK = 128
B_HEADS = 8
Q_RANK = 768
KV_RANK = 512
DH_NOPE = 128
DH_ROPE = 64
DH_V = 128
ROPE_THETA = 10000.0
SM_SCALE = (DH_NOPE + DH_ROPE) ** -0.5
C_WIDTH = 1024
CONV_W = 3
POOL_WINDOWS = (2, 4, 8, 16)
D_WIDTH = 1024
D_GROUP = D_WIDTH // len(POOL_WINDOWS)
MAX_WIN = max(POOL_WINDOWS)
N_GROUPS = 4
EXPERTS_PER_GROUP = 8
N_EXPERTS = N_GROUPS * EXPERTS_PER_GROUP
TOP_K = 2
D_EXPERT = 512
ALPHA = (2 * DEPTH) ** 0.25
LN_EPS = 1e-5
RMS_EPS = 1e-6

LANES = 128
HEAD_PAD = 2 * LANES
IN_EVEN_PAD = 2 * A_WIDTH + Q_RANK + KV_RANK + LANES
VMEM_LIMIT = 56 * 1024 * 1024

ROW_TILE = 256
MM_TM = 512
MM_TK = 256
ATT_BQ = 256
ATT_BK = 256
PAGES_PER_STEP = 16
MOE_BM = 256
HALO = 16


def _cparams(sem):
    return pltpu.CompilerParams(dimension_semantics=sem, vmem_limit_bytes=VMEM_LIMIT)


def _ln(x, g, b):
    mu = jnp.mean(x, axis=-1, keepdims=True)
    xc = x - mu
    var = jnp.mean(xc * xc, axis=-1, keepdims=True)
    return xc * lax.rsqrt(var + LN_EPS) * g + b


def _rms(x, g):
    return x * lax.rsqrt(jnp.mean(x * x, axis=-1, keepdims=True) + RMS_EPS) * g


def _dot(a, b):
    return jnp.dot(a, b, preferred_element_type=jnp.float32)


def _dot_nt(a, b):
    return lax.dot_general(a, b, (((1,), (1,)), ((), ())), preferred_element_type=jnp.float32)


def _mm_kernel(x_ref, w_ref, o_ref, acc_ref):
    k = pl.program_id(1)

    @pl.when(k == 0)
    def _():
        acc_ref[...] = jnp.zeros_like(acc_ref)

    acc_ref[...] += _dot(x_ref[...], w_ref[...])

    @pl.when(k == pl.num_programs(1) - 1)
    def _():
        o_ref[...] = acc_ref[...]


def _mm_ln_kernel(x_ref, w_ref, r_ref, g_ref, b_ref, o_ref, acc_ref):
    k = pl.program_id(1)

    @pl.when(k == 0)
    def _():
        acc_ref[...] = jnp.zeros_like(acc_ref)

    acc_ref[...] += _dot(x_ref[...], w_ref[...])

    @pl.when(k == pl.num_programs(1) - 1)
    def _():
        o_ref[...] = _ln(ALPHA * r_ref[...] + acc_ref[...], g_ref[...], b_ref[...])


def _matmul(x, w, *, resid=None, g=None, b=None, rows=None, name="mm"):
    m, kdim = x.shape
    m = m if rows is None else rows
    n = w.shape[1]
    tm = MM_TM
    tk = min(MM_TK, kdim)
    grid = (m // tm, kdim // tk)
    in_specs = [pl.BlockSpec((tm, tk), lambda i, k: (i, k)),
                pl.BlockSpec((tk, n), lambda i, k: (k, 0))]
    args = [x, w]
    kern = _mm_kernel
    if resid is not None:
        in_specs += [pl.BlockSpec((tm, n), lambda i, k: (i, 0)),
                     pl.BlockSpec((1, n), lambda i, k: (0, 0)),
                     pl.BlockSpec((1, n), lambda i, k: (0, 0))]
        args += [resid, g.reshape(1, n), b.reshape(1, n)]
        kern = _mm_ln_kernel
    return pl.pallas_call(
        kern,
        out_shape=jax.ShapeDtypeStruct((m, n), jnp.float32),
        grid=grid,
        in_specs=in_specs,
        out_specs=pl.BlockSpec((tm, n), lambda i, k: (i, 0)),
        scratch_shapes=[pltpu.VMEM((tm, n), jnp.float32)],
        compiler_params=_cparams(("parallel", "arbitrary")),
        name=name,
    )(*args)


def _head_mm_kernel(x_ref, w_ref, o_ref):
    o_ref[...] = _dot(x_ref[...], w_ref[0])


def _head_mm(x, w, *, x_block, x_stride, name):
    m = x.shape[0]
    nh, kdim, n = w.shape
    assert kdim == x_block
    return pl.pallas_call(
        _head_mm_kernel,
        out_shape=jax.ShapeDtypeStruct((m, nh * n), jnp.float32),
        grid=(nh,),
        in_specs=[pl.BlockSpec((m, x_block), lambda h: (0, x_stride * h)),
                  pl.BlockSpec((1, kdim, n), lambda h: (h, 0, 0))],
        out_specs=pl.BlockSpec((m, n), lambda h: (0, h)),
        compiler_params=_cparams(("parallel",)),
        name=name,
    )(x, w)


def _rope128(r, c, s1, s2):
    return r * c + pltpu.roll(r, LANES - DH_ROPE // 2, axis=1) * s1 + pltpu.roll(r, DH_ROPE // 2, axis=1) * s2


def _even_post_kernel(h_ref, wm_ref, bias_ref, lng_ref, lnb_ref, qg_ref, wq_ref, kvg_ref,
                      c_ref, s1_ref, s2_ref,
                      ya_ref, vn_ref, q_ref, ckv_ref, kpe_ref):
    s_u, s_v, s_q, s_kv = A_WIDTH, 2 * A_WIDTH, 2 * A_WIDTH + Q_RANK, 2 * A_WIDTH + Q_RANK + KV_RANK
    u = jax.nn.gelu(h_ref[:, 0:s_u])
    vn = _ln(jax.nn.gelu(h_ref[:, s_u:s_v]), lng_ref[...], lnb_ref[...])
    vn_ref[...] = vn
    for c in range(ROW_TILE // CHUNK):
        rows = slice(c * CHUNK, (c + 1) * CHUNK)
        for hh in range(A_HEADS):
            cols = slice(hh * HEAD_DIM, (hh + 1) * HEAD_DIM)
            mixed = _dot(wm_ref[0, hh], vn[rows, cols]) + bias_ref[0, :, cols]
            ya_ref[rows, cols] = u[rows, cols] * mixed
    qn = _rms(h_ref[:, s_v:s_q], qg_ref[...])
    q = _dot(qn, wq_ref[...])
    cc, s1, s2 = c_ref[...], s1_ref[...], s2_ref[...]
    for hh in range(B_HEADS):
        base = hh * HEAD_PAD
        q_ref[:, base:base + LANES] = q[:, base:base + LANES]
        q_ref[:, base + LANES:base + HEAD_PAD] = _rope128(q[:, base + LANES:base + HEAD_PAD], cc, s1, s2)
    ckv_ref[...] = _rms(h_ref[:, s_q:s_kv], kvg_ref[...])
    kpe_ref[...] = _rope128(h_ref[:, s_kv:s_kv + LANES], cc, s1, s2)


def _even_post(h, wm, bias, ln_g, ln_b, q_g, w_q2, kv_g, rope_c, rope_s1, rope_s2, n_prompt_tiles):
    n = h.shape[0]
    tm = ROW_TILE
    row = lambda w: pl.BlockSpec((tm, w), lambda i: (i, 0))
    full = lambda a: pl.BlockSpec(a.shape, lambda i: (0,) * a.ndim)
    sel = lambda i: jnp.where(i >= n_prompt_tiles, 1, 0)
    outs = (A_WIDTH, A_WIDTH, B_HEADS * HEAD_PAD, KV_RANK, LANES)
    return pl.pallas_call(
        _even_post_kernel,
        out_shape=tuple(jax.ShapeDtypeStruct((n, w), jnp.float32) for w in outs),
        grid=(n // tm,),
        in_specs=[row(IN_EVEN_PAD),
                  pl.BlockSpec((1, A_HEADS, CHUNK, CHUNK), lambda i: (sel(i), 0, 0, 0)),
                  pl.BlockSpec((1, CHUNK, A_WIDTH), lambda i: (sel(i), 0, 0)),
                  full(ln_g), full(ln_b), full(q_g), full(w_q2), full(kv_g),
                  row(LANES), row(LANES), row(LANES)],
        out_specs=tuple(row(w) for w in outs),
        compiler_params=_cparams(("parallel",)),
        name="even_post",
    )(h, wm, bias, ln_g, ln_b, q_g, w_q2, kv_g, rope_c, rope_s1, rope_s2)


def _prompt_attn_kernel(q_ref, kn_ref, kpe_ref, v_ref, o_ref, m_ref, l_ref, acc_ref):
    qi = pl.program_id(1)
    kj = pl.program_id(2)

    @pl.when(kj == 0)
    def _():
        m_ref[...] = jnp.full_like(m_ref, -jnp.inf)
        l_ref[...] = jnp.zeros_like(l_ref)
        acc_ref[...] = jnp.zeros_like(acc_ref)

    @pl.when(kj <= qi)
    def _():
        q_pos = qi * ATT_BQ + lax.broadcasted_iota(jnp.int32, (ATT_BQ, ATT_BK), 0)
        k_pos = kj * ATT_BK + lax.broadcasted_iota(jnp.int32, (ATT_BQ, ATT_BK), 1)
        causal = k_pos <= q_pos
        kpe = kpe_ref[...]
        for hh in range(B_HEADS):
            cols = slice(hh * DH_V, (hh + 1) * DH_V)
            kh = jnp.concatenate([kn_ref[:, cols], kpe], axis=1)
            s = _dot_nt(q_ref[:, hh * HEAD_PAD:(hh + 1) * HEAD_PAD], kh) * SM_SCALE
            s = jnp.where(causal, s, -jnp.inf)
            m_prev = m_ref[hh]
            m_new = jnp.maximum(m_prev, jnp.max(s, axis=1, keepdims=True))
            a = jnp.exp(m_prev - m_new)
            p = jnp.exp(s - m_new)
            l_ref[hh] = a * l_ref[hh] + jnp.sum(p, axis=1, keepdims=True)
            acc_ref[:, cols] = a * acc_ref[:, cols] + _dot(p, v_ref[:, cols])
            m_ref[hh] = m_new

    @pl.when(kj == pl.num_programs(2) - 1)
    def _():
        for hh in range(B_HEADS):
            cols = slice(hh * DH_V, (hh + 1) * DH_V)
            o_ref[:, cols] = acc_ref[:, cols] / l_ref[hh]


def _prompt_attn(q, kv_exp, kpe, n_batch, seq):
    nq = seq // ATT_BQ
    nk = seq // ATT_BK
    hv = B_HEADS * DH_V
    kblk = lambda b, i, j: b * nk + jnp.minimum(j, i)
    return pl.pallas_call(
        _prompt_attn_kernel,
        out_shape=jax.ShapeDtypeStruct((n_batch * seq, hv), jnp.float32),
        grid=(n_batch, nq, nk),
        in_specs=[pl.BlockSpec((ATT_BQ, B_HEADS * HEAD_PAD), lambda b, i, j: (b * nq + i, 0)),
                  pl.BlockSpec((ATT_BK, hv), lambda b, i, j: (kblk(b, i, j), 0)),
                  pl.BlockSpec((ATT_BK, LANES), lambda b, i, j: (kblk(b, i, j), 0)),
                  pl.BlockSpec((ATT_BK, hv), lambda b, i, j: (kblk(b, i, j), 1))],
        out_specs=pl.BlockSpec((ATT_BQ, hv), lambda b, i, j: (b * nq + i, 0)),
        scratch_shapes=[pltpu.VMEM((B_HEADS, ATT_BQ, 1), jnp.float32),
                        pltpu.VMEM((B_HEADS, ATT_BQ, 1), jnp.float32),
                        pltpu.VMEM((ATT_BQ, hv), jnp.float32)],
        compiler_params=_cparams(("parallel", "parallel", "arbitrary")),
        name="prompt_attn",
    )(q, kv_exp, kpe, kv_exp)


def _decode_attn_kernel(pt_ref, qa_ref, qp_ref, cn_ref, kn_ref, *rest, n_steps):
    kv_refs = rest[:PAGES_PER_STEP]
    pe_refs = rest[PAGES_PER_STEP:2 * PAGES_PER_STEP]
    o_ref, m_ref, l_ref, acc_ref = rest[2 * PAGES_PER_STEP:]
    c = pl.program_id(1)

    @pl.when(c == 0)
    def _():
        m_ref[...] = jnp.full_like(m_ref, -jnp.inf)
        l_ref[...] = jnp.zeros_like(l_ref)
        acc_ref[...] = jnp.zeros_like(acc_ref)

    qa = qa_ref[0]
    qp = qp_ref[0][:, :DH_ROPE]

    def update(s_list, v_list):
        m_prev = m_ref[...]
        m_new = m_prev
        for s in s_list:
            m_new = jnp.maximum(m_new, jnp.max(s, axis=1, keepdims=True))
        a = jnp.exp(m_prev - m_new)
        l_new = a * l_ref[...]
        acc = a * acc_ref[...]
        for s, v in zip(s_list, v_list):
            p = jnp.exp(s - m_new)
            l_new = l_new + jnp.sum(p, axis=1, keepdims=True)
            acc = acc + _dot(p, v)
        m_ref[...] = m_new
        l_ref[...] = l_new
        acc_ref[...] = acc

    s_list = [(_dot_nt(qa, kv_refs[j][...]) + _dot_nt(qp, pe_refs[j][...])) * SM_SCALE
              for j in range(PAGES_PER_STEP)]
    update(s_list, [kv_refs[j][...] for j in range(PAGES_PER_STEP)])

    @pl.when(c == n_steps - 1)
    def _():
        cn = cn_ref[0]
        t_new = cn.shape[0]
        s = (_dot_nt(qa, cn) + _dot_nt(qp, kn_ref[0][:, :DH_ROPE])) * SM_SCALE
        q_t = lax.broadcasted_iota(jnp.int32, s.shape, 0) // B_HEADS
        k_t = lax.broadcasted_iota(jnp.int32, s.shape, 1)
        s = jnp.where(k_t <= q_t, s, -jnp.inf)
        update([s], [cn])
        o_ref[0] = acc_ref[...] / l_ref[...]


def _decode_attn(page_table, q_abs, q_pe, ckv_new, kpe_new, cache_kv, cache_pe, layer):
    nb, n_pages = page_table.shape
    n_steps = n_pages // PAGES_PER_STEP
    rows = q_abs.shape[1]
    t_new = ckv_new.shape[1]

    def page_spec(width, j):
        return pl.BlockSpec((None, None, PAGE_SIZE, width),
                            lambda b, c, pt: (layer, pt[b * n_pages + c * PAGES_PER_STEP + j], 0, 0))

    per_b = lambda r, w: pl.BlockSpec((1, r, w), lambda b, c, pt: (b, 0, 0))
    grid_spec = pltpu.PrefetchScalarGridSpec(
        num_scalar_prefetch=1,
        grid=(nb, n_steps),
        in_specs=[per_b(rows, KV_RANK), per_b(rows, LANES), per_b(t_new, KV_RANK), per_b(t_new, LANES)]
                 + [page_spec(KV_RANK, j) for j in range(PAGES_PER_STEP)]
                 + [page_spec(DH_ROPE, j) for j in range(PAGES_PER_STEP)],
        out_specs=per_b(rows, KV_RANK),
        scratch_shapes=[pltpu.VMEM((rows, 1), jnp.float32),
                        pltpu.VMEM((rows, 1), jnp.float32),
                        pltpu.VMEM((rows, KV_RANK), jnp.float32)],
    )
    return pl.pallas_call(
        partial(_decode_attn_kernel, n_steps=n_steps),
        out_shape=jax.ShapeDtypeStruct((nb, rows, KV_RANK), jnp.float32),
        grid_spec=grid_spec,
        compiler_params=_cparams(("parallel", "arbitrary")),
        name="decode_attn",
    )(page_table.reshape(-1), q_abs, q_pe, ckv_new, kpe_new,
      *([cache_kv] * PAGES_PER_STEP), *([cache_pe] * PAGES_PER_STEP))


def _pool_conv_tail(z_ext, d_ext, d_cur, cb, t0, n_rows, n_valid_ref, cw_ref, pw_ref, ps_ref, mix_ref):
    conv = z_ext[HALO - 2:HALO - 2 + n_rows] * cw_ref[0:1]
    for k in range(1, CONV_W):
        conv = conv + z_ext[HALO - 2 + k:HALO - 2 + k + n_rows] * cw_ref[k:k + 1]
    mix_ref[:, 0:C_WIDTH] = cb * conv
    sums = {1: d_ext}
    w = 1
    while w < MAX_WIN:
        prev = sums[w]
        sums[2 * w] = prev[w:] + prev[:-w]
        w *= 2
    t_idx = t0 + lax.broadcasted_iota(jnp.int32, (n_rows, 1), 0)
    for g, win in enumerate(POOL_WINDOWS):
        cols = slice(g * D_GROUP, (g + 1) * D_GROUP)
        s = sums[win]
        win_sum = s[s.shape[0] - n_rows:, cols]
        count = jnp.minimum(n_valid_ref + t_idx + 1, win).astype(jnp.float32)
        diff = win_sum / count - d_cur[:, cols]
        mix_ref[:, C_WIDTH + g * D_GROUP:C_WIDTH + (g + 1) * D_GROUP] = _dot(diff, pw_ref[g]) * ps_ref[:, cols]


def _odd_prompt_kernel(h_ref, halo_ref, cw_ref, pw_ref, ps_ref, mix_ref, z_ref, *, tiles_per_seq):
    i = pl.program_id(0)
    first = (i % tiles_per_seq) == 0
    c1, c2, c3 = C_WIDTH, 2 * C_WIDTH, 3 * C_WIDTH
    z = h_ref[:, c1:c2] * h_ref[:, c2:c3]
    d = h_ref[:, c3:c3 + D_WIDTH]
    z_halo = jnp.where(first, 0.0, halo_ref[:, c1:c2] * halo_ref[:, c2:c3])
    d_halo = jnp.where(first, 0.0, halo_ref[:, c3:c3 + D_WIDTH])
    z_ext = jnp.concatenate([z_halo, z], axis=0)
    d_ext = jnp.concatenate([d_halo, d], axis=0)
    t0 = (i % tiles_per_seq) * ROW_TILE
    _pool_conv_tail(z_ext, d_ext, d, h_ref[:, 0:c1], t0, ROW_TILE, 0, cw_ref, pw_ref, ps_ref, mix_ref)
    z_ref[0] = z[ROW_TILE - 8:]


def _odd_prompt(h, n, conv_w, pool_w, pool_scale, seq):
    tm = ROW_TILE
    tiles_per_seq = seq // tm
    full = lambda a: pl.BlockSpec(a.shape, lambda i: (0,) * a.ndim)
    return pl.pallas_call(
        partial(_odd_prompt_kernel, tiles_per_seq=tiles_per_seq),
        out_shape=(jax.ShapeDtypeStruct((n, C_WIDTH + D_WIDTH), jnp.float32),
                   jax.ShapeDtypeStruct((n // tm, 8, C_WIDTH), jnp.float32)),
        grid=(n // tm,),
        in_specs=[pl.BlockSpec((tm, h.shape[1]), lambda i: (i, 0)),
                  pl.BlockSpec((HALO, h.shape[1]), lambda i: (jnp.maximum(i * (tm // HALO) - 1, 0), 0)),
                  full(conv_w), full(pool_w), full(pool_scale)],
        out_specs=(pl.BlockSpec((tm, C_WIDTH + D_WIDTH), lambda i: (i, 0)),
                   pl.BlockSpec((1, 8, C_WIDTH), lambda i: (i, 0, 0))),
        compiler_params=_cparams(("parallel",)),
        name="odd_prompt",
    )(h, h, conv_w, pool_w, pool_scale)


def _odd_sample_kernel(h_ref, sc_ref, sp_ref, cw_ref, pw_ref, ps_ref, mix_ref, z_ref, *, n_valid, t_new):
    c1, c2, c3 = C_WIDTH, 2 * C_WIDTH, 3 * C_WIDTH
    cw = cw_ref[...]
    z = [h_ref[t][:, c1:c2] * h_ref[t][:, c2:c3] for t in range(t_new)]
    zp = [sc_ref[k] for k in range(CONV_W - 1)] + z
    pp = [sp_ref[k] for k in range(MAX_WIN - 1)] + [h_ref[t][:, c3:c3 + D_WIDTH] for t in range(t_new)]
    for t in range(t_new):
        conv = zp[t] * cw[0:1]
        for k in range(1, CONV_W):
            conv = conv + zp[t + k] * cw[k:k + 1]
        mix_ref[t, :, 0:C_WIDTH] = h_ref[t][:, 0:c1] * conv
        z_ref[t] = z[t]
        cur = MAX_WIN - 1 + t
        for g, win in enumerate(POOL_WINDOWS):
            cols = slice(g * D_GROUP, (g + 1) * D_GROUP)
            win_sum = pp[cur][:, cols]
            for k in range(1, win):
                win_sum = win_sum + pp[cur - k][:, cols]
            count = float(min(n_valid + t + 1, win))
            diff = win_sum / count - pp[cur][:, cols]
            mix_ref[t, :, C_WIDTH + g * D_GROUP:C_WIDTH + (g + 1) * D_GROUP] = (
                _dot(diff, pw_ref[g]) * ps_ref[:, cols])


def _odd_sample(h_tm, conv_state_tm, pool_state_tm, conv_w, pool_w, pool_scale, n_valid):
    t_new, nb, width = h_tm.shape
    bb = 32
    full = lambda a: pl.BlockSpec(a.shape, lambda i: (0,) * a.ndim)
    tm_spec = lambda t, w: pl.BlockSpec((t, bb, w), lambda i: (0, i, 0))
    return pl.pallas_call(
        partial(_odd_sample_kernel, n_valid=n_valid, t_new=t_new),
        out_shape=(jax.ShapeDtypeStruct((t_new, nb, C_WIDTH + D_WIDTH), jnp.float32),
                   jax.ShapeDtypeStruct((t_new, nb, C_WIDTH), jnp.float32)),
        grid=(nb // bb,),
        in_specs=[tm_spec(t_new, width), tm_spec(CONV_W - 1, C_WIDTH), tm_spec(MAX_WIN - 1, D_WIDTH),
                  full(conv_w), full(pool_w), full(pool_scale)],
        out_specs=(tm_spec(t_new, C_WIDTH + D_WIDTH), tm_spec(t_new, C_WIDTH)),
        compiler_params=_cparams(("parallel",)),
        name="odd_sample",
    )(h_tm, conv_state_tm, pool_state_tm, conv_w, pool_w, pool_scale)


def _router_kernel(x_ref, w_ref, b_ref, e_ref, g_ref):
    logits = jnp.dot(x_ref[...], w_ref[...], preferred_element_type=jnp.float32,
                     precision=lax.Precision.HIGHEST) + b_ref[...]
    lane = lax.broadcasted_iota(jnp.int32, logits.shape, 1).astype(jnp.float32)
    big = float(LANES)
    neg = -jnp.inf
    is_c = lane < N_GROUPS
    lc = jnp.where(is_c, logits, neg)
    ec = jnp.exp(lc - jnp.max(lc, axis=1, keepdims=True))
    pc = ec / jnp.sum(ec, axis=1, keepdims=True)
    p_g = jnp.max(pc, axis=1, keepdims=True)
    g_sel = jnp.min(jnp.where(is_c & (pc == p_g), lane, big), axis=1, keepdims=True)
    lo = N_GROUPS + g_sel * EXPERTS_PER_GROUP
    in_g = (lane >= lo) & (lane < lo + EXPERTS_PER_GROUP)
    lf = jnp.where(in_g, logits, neg)
    ef = jnp.exp(lf - jnp.max(lf, axis=1, keepdims=True))
    pf = ef / jnp.sum(ef, axis=1, keepdims=True)
    p1 = jnp.max(pf, axis=1, keepdims=True)
    j1 = jnp.min(jnp.where(in_g & (pf == p1), lane, big), axis=1, keepdims=True)
    rest = in_g & (lane != j1)
    pf2 = jnp.where(rest, pf, -1.0)
    p2 = jnp.max(pf2, axis=1, keepdims=True)
    j2 = jnp.min(jnp.where(rest & (pf2 == p2), lane, big), axis=1, keepdims=True)
    tot = p1 + p2
    e_ref[...] = jnp.where(lane == 0, j1 - N_GROUPS, jnp.where(lane == 1, j2 - N_GROUPS, 0.0)).astype(jnp.int32)
    g_ref[...] = jnp.where(lane == 0, p_g * p1 / tot, jnp.where(lane == 1, p_g * p2 / tot, 0.0))


def _router(x, w, b):
    n, d = x.shape
    tm = MM_TM
    return pl.pallas_call(
        _router_kernel,
        out_shape=(jax.ShapeDtypeStruct((n, LANES), jnp.int32),
                   jax.ShapeDtypeStruct((n, LANES), jnp.float32)),
        grid=(n // tm,),
        in_specs=[pl.BlockSpec((tm, d), lambda i: (i, 0)),
                  pl.BlockSpec((d, LANES), lambda i: (0, 0)),
                  pl.BlockSpec((1, LANES), lambda i: (0, 0))],
        out_specs=(pl.BlockSpec((tm, LANES), lambda i: (i, 0)),
                   pl.BlockSpec((tm, LANES), lambda i: (i, 0))),
        compiler_params=_cparams(("parallel",)),
        name="router",
    )(x, w, b)


def _expert_kernel(be_ref, nu_ref, x_ref, wg_ref, wu_ref, wd_ref, o_ref):
    blk = pl.program_id(0)

    @pl.when(blk < nu_ref[0])
    def _():
        x = x_ref[...]
        hid = jax.nn.silu(_dot(x, wg_ref[0])) * _dot(x, wu_ref[0])
        o_ref[...] = _dot(hid, wd_ref[0])

    @pl.when(blk >= nu_ref[0])
    def _():
        o_ref[...] = jnp.zeros_like(o_ref)


def _expert_ffn(block_e, n_used, xb, w_gate, w_up, w_down):
    rows, d = xb.shape
    n_blocks = rows // MOE_BM
    grid_spec = pltpu.PrefetchScalarGridSpec(
        num_scalar_prefetch=2,
        grid=(n_blocks,),
        in_specs=[pl.BlockSpec((MOE_BM, d), lambda i, be, nu: (i, 0)),
                  pl.BlockSpec((1, d, D_EXPERT), lambda i, be, nu: (be[i], 0, 0)),
                  pl.BlockSpec((1, d, D_EXPERT), lambda i, be, nu: (be[i], 0, 0)),
                  pl.BlockSpec((1, D_EXPERT, d), lambda i, be, nu: (be[i], 0, 0))],
        out_specs=pl.BlockSpec((MOE_BM, d), lambda i, be, nu: (i, 0)),
    )
    return pl.pallas_call(
        _expert_kernel,
        out_shape=jax.ShapeDtypeStruct((rows, d), jnp.float32),
        grid_spec=grid_spec,
        compiler_params=_cparams(("arbitrary",)),
        name="expert_ffn",
    )(block_e, n_used, xb, w_gate, w_up, w_down)


def _combine_ln_kernel(x_ref, y0_ref, y1_ref, gate_ref, g_ref, b_ref, o_ref):
    gate = gate_ref[...]
    moe = gate[:, 0:1] * y0_ref[...] + gate[:, 1:2] * y1_ref[...]
    o_ref[...] = _ln(ALPHA * x_ref[...] + moe, g_ref[...], b_ref[...])


def _combine_ln(x, y0, y1, gates, g, b):
    n, d = x.shape
    tm = MM_TM
    row = lambda w: pl.BlockSpec((tm, w), lambda i: (i, 0))
    vec = pl.BlockSpec((1, d), lambda i: (0, 0))
    return pl.pallas_call(
        _combine_ln_kernel,
        out_shape=jax.ShapeDtypeStruct((n, d), jnp.float32),
        grid=(n // tm,),
        in_specs=[row(d), row(d), row(d), row(LANES), vec, vec],
        out_specs=row(d),
        compiler_params=_cparams(("parallel",)),
        name="combine_ln",
    )(x, y0, y1, gates, g.reshape(1, d), b.reshape(1, d))


def _moe_ln(x, rc_w, rc_b, rf_w, rf_b, w_gate, w_up, w_down, ln_g, ln_b):
    n, d = x.shape
    pad = LANES - N_GROUPS - N_EXPERTS
    rw = jnp.concatenate([rc_w, rf_w, jnp.zeros((d, pad), jnp.float32)], axis=1)
    rb = jnp.concatenate([rc_b, rf_b, jnp.zeros((pad,), jnp.float32)]).reshape(1, LANES)
    e_pad, gates = _router(x, rw, rb)
    a = n * TOP_K
    flat_e = e_pad[:, :TOP_K].reshape(-1)
    flat_tok = jnp.repeat(jnp.arange(n, dtype=jnp.int32), TOP_K)
    order = jnp.argsort(flat_e)
    se, stok = flat_e[order], flat_tok[order]
    counts = jnp.zeros((N_EXPERTS,), jnp.int32).at[flat_e].add(1)
    padded = (counts + MOE_BM - 1) // MOE_BM * MOE_BM
    pad_end = jnp.cumsum(padded)
    pad_start = pad_end - padded
    start = jnp.cumsum(counts) - counts
    dest = pad_start[se] + jnp.arange(a, dtype=jnp.int32) - start[se]
    n_blocks = a // MOE_BM + N_EXPERTS
    rows = n_blocks * MOE_BM
    row_tok = jnp.zeros((rows,), jnp.int32).at[dest].set(stok)
    block_e = jnp.minimum(jnp.searchsorted(pad_end, jnp.arange(n_blocks, dtype=jnp.int32) * MOE_BM, side='right'),
                          N_EXPERTS - 1).astype(jnp.int32)
    n_used = (pad_end[-1:] // MOE_BM).astype(jnp.int32)
    pos = jnp.zeros((a,), jnp.int32).at[order].set(dest).reshape(n, TOP_K)
    xb = jnp.take(x, row_tok, axis=0)
    yb = _expert_ffn(block_e, n_used, xb, w_gate, w_up, w_down)
    y0 = jnp.take(yb, pos[:, 0], axis=0)
    y1 = jnp.take(yb, pos[:, 1], axis=0)
    return _combine_ln(x, y0, y1, gates, ln_g, ln_b)


def _rope_tables(pos):
    inv = ROPE_THETA ** (-jnp.arange(0, DH_ROPE, 2, dtype=jnp.float32) / DH_ROPE)
    ang = pos[:, None] * inv[None, :]
    cos, sin = jnp.cos(ang), jnp.sin(ang)
    z = jnp.zeros_like(cos)
    c = jnp.concatenate([cos, cos, z, z], axis=1)
    s1 = jnp.concatenate([-sin, z, z, z], axis=1)
    s2 = jnp.concatenate([z, sin, z, z], axis=1)
    return c, s1, s2


def _even_layer(x, n_prompt, bp, seq, bs, t_new, rope, page_table, cache_kv, cache_pe, layer,
                w_in, sgu_ln_g, sgu_ln_b, sgu_w, sgu_b, q_norm_g, w_q_up, kv_norm_g, w_uk, w_uv, w_out,
                ln_g, ln_b):
    w_in_p = jnp.pad(w_in, ((0, 0), (0, IN_EVEN_PAD - w_in.shape[1])))
    w_q2 = jnp.pad(w_q_up, ((0, 0), (0, 0), (0, HEAD_PAD - DH_NOPE - DH_ROPE))).reshape(Q_RANK, B_HEADS * HEAD_PAD)
    reps = CHUNK // t_new
    w_small = jnp.tril(sgu_w[:, :t_new, :t_new])
    eye = jnp.eye(reps, dtype=jnp.float32)
    wm_sample = jnp.einsum('ab,hij->haibj', eye, w_small).reshape(A_HEADS, CHUNK, CHUNK)
    wm = jnp.stack([jnp.tril(sgu_w), wm_sample])
    bias_p = jnp.repeat(sgu_b.T, HEAD_DIM, axis=1)
    bias_s = jnp.tile(jnp.repeat(sgu_b[:, :t_new].T, HEAD_DIM, axis=1), (reps, 1))
    bias = jnp.stack([bias_p, bias_s])
    w_kv_exp = jnp.concatenate([w_uk.reshape(KV_RANK, -1), w_uv.reshape(KV_RANK, -1)], axis=1)
    w_uk_t = jnp.transpose(w_uk, (1, 2, 0))
    w_uv_h = jnp.transpose(w_uv, (1, 0, 2))

    h = _matmul(x, w_in_p, name="in_even")
    y_a, vn, q, ckv, kpe = _even_post(
        h, wm, bias, sgu_ln_g.reshape(1, -1), sgu_ln_b.reshape(1, -1), q_norm_g.reshape(1, -1), w_q2,
        kv_norm_g.reshape(1, -1), *rope, n_prompt // ROW_TILE)

    kv_exp = _matmul(ckv, w_kv_exp, rows=n_prompt, name="kv_expand")
    o_p = _prompt_attn(q, kv_exp, kpe, bp, seq)

    q_s = q[n_prompt:]
    q_abs = _head_mm(q_s, w_uk_t, x_block=DH_NOPE, x_stride=HEAD_PAD // DH_NOPE, name="q_absorb")
    q_abs = q_abs.reshape(bs, t_new * B_HEADS, KV_RANK)
    q_pe = q_s.reshape(bs * t_new, B_HEADS, HEAD_PAD)[:, :, LANES:].reshape(bs, t_new * B_HEADS, LANES)
    pad_new = ((0, 0), (0, 8 - t_new), (0, 0))
    o_lat = _decode_attn(page_table, q_abs, q_pe, jnp.pad(ckv[n_prompt:].reshape(bs, t_new, KV_RANK), pad_new),
                         jnp.pad(kpe[n_prompt:].reshape(bs, t_new, LANES), pad_new), cache_kv, cache_pe, layer)
    o_s = _head_mm(o_lat.reshape(bs * t_new, B_HEADS * KV_RANK), w_uv_h, x_block=KV_RANK, x_stride=1,
                   name="o_up")

    mix = jnp.concatenate([y_a, jnp.concatenate([o_p, o_s], axis=0)], axis=1)
    x_new = _matmul(mix, w_out, resid=x, g=ln_g, b=ln_b, name="out_even")
    return x_new, ckv, kpe[:, :DH_ROPE], vn


def _odd_layer(x, n_prompt, bp, seq, bs, t_new, n_valid, conv_state, pool_state,
               w_in, conv_w, pool_w, pool_scale, w_out, ln_g, ln_b):
    h = _matmul(x, w_in, name="in_odd")
    ps = pool_scale.reshape(1, -1)
    mix_p, z_tail = _odd_prompt(h, n_prompt, conv_w, pool_w, ps, seq)
    h_s = h[n_prompt:].reshape(bs, t_new, -1)
    mix_s, z_s = _odd_sample(jnp.swapaxes(h_s, 0, 1), jnp.swapaxes(conv_state, 0, 1),
                             jnp.swapaxes(pool_state, 0, 1), conv_w, pool_w, ps, n_valid)
    mix = jnp.concatenate([mix_p, jnp.swapaxes(mix_s, 0, 1).reshape(bs * t_new, -1)], axis=0)
    x_new = _matmul(mix, w_out, resid=x, g=ln_g, b=ln_b, name="out_odd")
    tiles_per_seq = seq // ROW_TILE
    conv_p = z_tail.reshape(bp, tiles_per_seq, 8, C_WIDTH)[:, -1, 8 - (CONV_W - 1):]
    d_p = h[:n_prompt, 3 * C_WIDTH:].reshape(bp, seq, D_WIDTH)
    pool_p = d_p[:, seq - (MAX_WIN - 1):]
    zp_s = jnp.concatenate([conv_state, jnp.swapaxes(z_s, 0, 1)], axis=1)
    conv_s = zp_s[:, -(CONV_W - 1):]
    pp_s = jnp.concatenate([pool_state, h_s[:, :, 3 * C_WIDTH:]], axis=1)
    pool_s = pp_s[:, -(MAX_WIN - 1):]
    return x_new, conv_p, pool_p, conv_s, pool_s


def kernel(x_prompt, x_sample, cache_kv_latent, cache_k_rope, state_conv, state_pool, page_table,
           w_in_even, sgu_ln_g, sgu_ln_b, sgu_w, sgu_b, q_norm_g, w_q_up, kv_norm_g, w_uk, w_uv, w_out_even,
           w_in_odd, conv_w, pool_w, pool_scale, w_out_odd,
           ln_mix_g, ln_mix_b, ln_ffn_g, ln_ffn_b,
           router_coarse_w, router_coarse_b, router_fine_w, router_fine_b,
           w_exp_gate, w_exp_up, w_exp_down):
    bp, seq, d = x_prompt.shape
    bs, t_new, _ = x_sample.shape
    n_prompt = bp * seq
    past_len = page_table.shape[1] * PAGE_SIZE
    x = jnp.concatenate([x_prompt.reshape(n_prompt, d), x_sample.reshape(bs * t_new, d)], axis=0)
    pos = jnp.concatenate([jnp.tile(jnp.arange(seq, dtype=jnp.float32), bp),
                           jnp.tile(past_len + jnp.arange(t_new, dtype=jnp.float32), bs)])
    rope = _rope_tables(pos)

    lat_p, pe_p, conv_p, pool_p = [], [], [], []
    lat_s, pe_s, v_s, conv_s, pool_s = [], [], [], [], []
    for l in range(DEPTH):
        i = l // 2
        if l % 2 == 0:
            x, ckv, kpe, vn = _even_layer(
                x, n_prompt, bp, seq, bs, t_new, rope, page_table, cache_kv_latent, cache_k_rope, i,
                w_in_even[i], sgu_ln_g[i], sgu_ln_b[i], sgu_w[i], sgu_b[i], q_norm_g[i], w_q_up[i],
                kv_norm_g[i], w_uk[i], w_uv[i], w_out_even[i], ln_mix_g[l], ln_mix_b[l])
            lat_p.append(ckv[:n_prompt].reshape(bp, seq, KV_RANK))
            pe_p.append(kpe[:n_prompt].reshape(bp, seq, DH_ROPE))
            lat_s.append(ckv[n_prompt:].reshape(bs, t_new, KV_RANK))
            pe_s.append(kpe[n_prompt:].reshape(bs, t_new, DH_ROPE))
            v_s.append(vn[n_prompt:].reshape(bs, t_new, A_WIDTH))
        else:
            x, cp, pp, cs, ps = _odd_layer(
                x, n_prompt, bp, seq, bs, t_new, past_len, state_conv[i], state_pool[i],
                w_in_odd[i], conv_w[i], pool_w[i], pool_scale[i], w_out_odd[i], ln_mix_g[l], ln_mix_b[l])
            conv_p.append(cp)
            pool_p.append(pp)
            conv_s.append(cs)
            pool_s.append(ps)
        x = _moe_ln(x, router_coarse_w[l], router_coarse_b[l], router_fine_w[l], router_fine_b[l],
                    w_exp_gate[l], w_exp_up[l], w_exp_down[l], ln_ffn_g[l], ln_ffn_b[l])
    return (x[:n_prompt].reshape(bp, seq, d), x[n_prompt:].reshape(bs, t_new, d),
            jnp.stack(lat_p), jnp.stack(pe_p), jnp.stack(conv_p), jnp.stack(pool_p),
            jnp.stack(lat_s), jnp.stack(pe_s), jnp.stack(v_s), jnp.stack(conv_s), jnp.stack(pool_s))
```

```python
from functools import partial

import jax
import jax.numpy as jnp
import numpy as np
from jax import lax
from jax.experimental import pallas as pl
from jax.experimental.pallas import tpu as pltpu

D_MODEL = 2048
DEPTH = 4
PAGE_SIZE = 128
HEAD_DIM = 128
A_HEADS = 8
A_WIDTH = A_HEADS * HEAD_DIM
CHUNK = 128
B_HEADS = 8
Q_RANK = 768
KV_RANK = 512
DH_NOPE = 128
DH_ROPE = 64
DH_V = 128
ROPE_THETA = 10000.0
SM_SCALE = (DH_NOPE + DH_ROPE) ** -0.5
C_WIDTH = 1024
CONV_W = 3
POOL_WINDOWS = (2, 4, 8, 16)
D_WIDTH = 1024
D_GROUP = D_WIDTH // len(POOL_WINDOWS)
MAX_WIN = max(POOL_WINDOWS)
N_GROUPS = 4
EXPERTS_PER_GROUP = 8
N_EXPERTS = N_GROUPS * EXPERTS_PER_GROUP
TOP_K = 2
D_EXPERT = 512
ALPHA = (2 * DEPTH) ** 0.25
LN_EPS = 1e-5
RMS_EPS = 1e-6

LANES = 128
HEAD_PAD = 2 * LANES
IN_EVEN_PAD = 2 * A_WIDTH + Q_RANK + KV_RANK + LANES
VMEM_LIMIT = 56 * 1024 * 1024

ROW_TILE = 256
MM_TM = 512
ATT_BQ = 256
ATT_BK = 256
PAGES_PER_STEP = 16
MOE_BM = 256
HALO = 16


def _cparams(sem):
    return pltpu.CompilerParams(dimension_semantics=sem, vmem_limit_bytes=VMEM_LIMIT)


def _ln(x, g, b):
    mu = jnp.mean(x, axis=-1, keepdims=True)
    xc = x - mu
    var = jnp.mean(xc * xc, axis=-1, keepdims=True)
    return xc * lax.rsqrt(var + LN_EPS) * g + b


def _rms(x, g):
    return x * lax.rsqrt(jnp.mean(x * x, axis=-1, keepdims=True) + RMS_EPS) * g


def _dot(a, b):
    return jnp.dot(a, b, preferred_element_type=jnp.float32)


def _dot_nt(a, b):
    return lax.dot_general(a, b, (((1,), (1,)), ((), ())), preferred_element_type=jnp.float32)


def _mm_kernel(x_ref, w_ref, o_ref):
    o_ref[...] = _dot(x_ref[...], w_ref[...])


def _mm_ln_kernel(x_ref, w_ref, r_ref, g_ref, b_ref, o_ref):
    o_ref[...] = _ln(ALPHA * r_ref[...] + _dot(x_ref[...], w_ref[...]), g_ref[...], b_ref[...])


def _matmul(x, w, *, tn=None, resid=None, g=None, b=None, rows=None, name="mm"):
    m, kdim = x.shape
    m = m if rows is None else rows
    n = w.shape[1]
    args = [x, w]
    if resid is None:
        tm, tn, kern = MM_TM, (n if tn is None else tn), _mm_kernel
        extra = []
    else:
        tm, tn, kern = ROW_TILE, n, _mm_ln_kernel
        extra = [pl.BlockSpec((tm, n), lambda j, i: (i, 0)),
                 pl.BlockSpec((1, n), lambda j, i: (0, 0)),
                 pl.BlockSpec((1, n), lambda j, i: (0, 0))]
        args += [resid, g.reshape(1, n), b.reshape(1, n)]
    return pl.pallas_call(
        kern,
        out_shape=jax.ShapeDtypeStruct((m, n), jnp.float32),
        grid=(n // tn, m // tm),
        in_specs=[pl.BlockSpec((tm, kdim), lambda j, i: (i, 0)),
                  pl.BlockSpec((kdim, tn), lambda j, i: (0, j))] + extra,
        out_specs=pl.BlockSpec((tm, tn), lambda j, i: (i, j)),
        compiler_params=_cparams(("parallel", "parallel")),
        name=name,
    )(*args)


def _head_mm_kernel(x_ref, w_ref, o_ref):
    o_ref[...] = _dot(x_ref[...], w_ref[0])


def _head_mm(x, w, *, x_block, x_stride, name):
    m = x.shape[0]
    nh, kdim, n = w.shape
    assert kdim == x_block
    return pl.pallas_call(
        _head_mm_kernel,
        out_shape=jax.ShapeDtypeStruct((m, nh * n), jnp.float32),
        grid=(nh,),
        in_specs=[pl.BlockSpec((m, x_block), lambda h: (0, x_stride * h)),
                  pl.BlockSpec((1, kdim, n), lambda h: (h, 0, 0))],
        out_specs=pl.BlockSpec((m, n), lambda h: (0, h)),
        compiler_params=_cparams(("parallel",)),
        name=name,
    )(x, w)


def _rope128(r, c, s1, s2):
    return r * c + pltpu.roll(r, LANES - DH_ROPE // 2, axis=1) * s1 + pltpu.roll(r, DH_ROPE // 2, axis=1) * s2


def _even_post_kernel(h_ref, wm_ref, bias_ref, lng_ref, lnb_ref, qg_ref, wq_ref, kvg_ref,
                      c_ref, s1_ref, s2_ref,
                      ya_ref, vn_ref, q_ref, ckv_ref, kpe_ref):
    s_u, s_v, s_q, s_kv = A_WIDTH, 2 * A_WIDTH, 2 * A_WIDTH + Q_RANK, 2 * A_WIDTH + Q_RANK + KV_RANK
    u = jax.nn.gelu(h_ref[:, 0:s_u])
    vn = _ln(jax.nn.gelu(h_ref[:, s_u:s_v]), lng_ref[...], lnb_ref[...])
    vn_ref[...] = vn
    for c in range(ROW_TILE // CHUNK):
        rows = slice(c * CHUNK, (c + 1) * CHUNK)
        for hh in range(A_HEADS):
            cols = slice(hh * HEAD_DIM, (hh + 1) * HEAD_DIM)
            mixed = _dot(wm_ref[0, hh], vn[rows, cols]) + bias_ref[0, :, cols]
            ya_ref[rows, cols] = u[rows, cols] * mixed
    ya_ref[:, A_WIDTH:] = jnp.zeros((ROW_TILE, ya_ref.shape[1] - A_WIDTH), jnp.float32)
    qn = _rms(h_ref[:, s_v:s_q], qg_ref[...])
    q = _dot(qn, wq_ref[...]) * SM_SCALE
    cc, s1, s2 = c_ref[...], s1_ref[...], s2_ref[...]
    for hh in range(B_HEADS):
        base = hh * HEAD_PAD
        q_ref[:, base:base + LANES] = q[:, base:base + LANES]
        q_ref[:, base + LANES:base + HEAD_PAD] = _rope128(q[:, base + LANES:base + HEAD_PAD], cc, s1, s2)
    ckv_ref[...] = _rms(h_ref[:, s_q:s_kv], kvg_ref[...])
    kpe_ref[...] = _rope128(h_ref[:, s_kv:s_kv + LANES], cc, s1, s2)


def _even_post(h, wm, bias, ln_g, ln_b, q_g, w_q2, kv_g, rope_c, rope_s1, rope_s2, n_prompt_tiles):
    n = h.shape[0]
    tm = ROW_TILE
    row = lambda w: pl.BlockSpec((tm, w), lambda i: (i, 0))
    full = lambda a: pl.BlockSpec(a.shape, lambda i: (0,) * a.ndim)
    sel = lambda i: jnp.where(i >= n_prompt_tiles, 1, 0)
    outs = (A_WIDTH + B_HEADS * DH_V, A_WIDTH, B_HEADS * HEAD_PAD, KV_RANK, LANES)
    return pl.pallas_call(
        _even_post_kernel,
        out_shape=tuple(jax.ShapeDtypeStruct((n, w), jnp.float32) for w in outs),
        grid=(n // tm,),
        in_specs=[row(IN_EVEN_PAD),
                  pl.BlockSpec((1, A_HEADS, CHUNK, CHUNK), lambda i: (sel(i), 0, 0, 0)),
                  pl.BlockSpec((1, CHUNK, A_WIDTH), lambda i: (sel(i), 0, 0)),
                  full(ln_g), full(ln_b), full(q_g), full(w_q2), full(kv_g),
                  row(LANES), row(LANES), row(LANES)],
        out_specs=tuple(row(w) for w in outs),
        compiler_params=_cparams(("parallel",)),
        name="even_post",
    )(h, wm, bias, ln_g, ln_b, q_g, w_q2, kv_g, rope_c, rope_s1, rope_s2)


def _prompt_attn_kernel(qi_ref, kj_ref, q_ref, kn_ref, kpe_ref, v_ref, mix_hbm, o_ref, m_ref, l_ref, acc_ref):
    del mix_hbm
    t = pl.program_id(1)
    qi = qi_ref[t]
    kj = kj_ref[t]

    @pl.when(kj == 0)
    def _():
        m_ref[...] = jnp.full_like(m_ref, -jnp.inf)
        l_ref[...] = jnp.zeros_like(l_ref)
        acc_ref[...] = jnp.zeros_like(acc_ref)

    def step(on_diagonal):
        kpe = kpe_ref[...]
        if on_diagonal:
            causal = (lax.broadcasted_iota(jnp.int32, (ATT_BQ, ATT_BK), 1)
                      <= lax.broadcasted_iota(jnp.int32, (ATT_BQ, ATT_BK), 0))
        for hh in range(B_HEADS):
            cols = slice(hh * DH_V, (hh + 1) * DH_V)
            kh = jnp.concatenate([kn_ref[:, cols], kpe], axis=1)
            s = _dot_nt(q_ref[:, hh * HEAD_PAD:(hh + 1) * HEAD_PAD], kh)
            if on_diagonal:
                s = jnp.where(causal, s, -jnp.inf)
            m_prev = m_ref[hh]
            m_new = jnp.maximum(m_prev, jnp.max(s, axis=1, keepdims=True))
            a = jnp.exp(m_prev - m_new)
            p = jnp.exp(s - jnp.concatenate([m_new] * (ATT_BK // LANES), axis=1))
            l_ref[hh] = a * l_ref[hh] + jnp.sum(p, axis=1, keepdims=True)
            acc_ref[:, cols] = a * acc_ref[:, cols] + _dot(p, v_ref[:, cols])
            m_ref[hh] = m_new

    @pl.when(kj < qi)
    def _():
        step(False)

    @pl.when(kj == qi)
    def _():
        step(True)
        for hh in range(B_HEADS):
            cols = slice(hh * DH_V, (hh + 1) * DH_V)
            o_ref[:, cols] = acc_ref[:, cols] / l_ref[hh]


def _prompt_attn(q, kv_exp, kpe, mix, n_batch, seq):
    assert ATT_BQ == ATT_BK
    nq = seq // ATT_BQ
    hv = B_HEADS * DH_V
    pairs = [(i, j) for i in range(nq) for j in range(i + 1)]
    qi_tbl = jnp.asarray(np.array([p[0] for p in pairs], np.int32))
    kj_tbl = jnp.asarray(np.array([p[1] for p in pairs], np.int32))
    grid_spec = pltpu.PrefetchScalarGridSpec(
        num_scalar_prefetch=2,
        grid=(n_batch, len(pairs)),
        in_specs=[pl.BlockSpec((ATT_BQ, B_HEADS * HEAD_PAD), lambda b, t, qi, kj: (b * nq + qi[t], 0)),
                  pl.BlockSpec((ATT_BK, hv), lambda b, t, qi, kj: (b * nq + kj[t], 0)),
                  pl.BlockSpec((ATT_BK, LANES), lambda b, t, qi, kj: (b * nq + kj[t], 0)),
                  pl.BlockSpec((ATT_BK, hv), lambda b, t, qi, kj: (b * nq + kj[t], 1)),
                  pl.BlockSpec(memory_space=pl.ANY)],
        out_specs=pl.BlockSpec((ATT_BQ, hv), lambda b, t, qi, kj: (b * nq + qi[t], 1)),
        scratch_shapes=[pltpu.VMEM((B_HEADS, ATT_BQ, LANES), jnp.float32),
                        pltpu.VMEM((B_HEADS, ATT_BQ, LANES), jnp.float32),
                        pltpu.VMEM((ATT_BQ, hv), jnp.float32)],
    )
    return pl.pallas_call(
        _prompt_attn_kernel,
        out_shape=jax.ShapeDtypeStruct(mix.shape, jnp.float32),
        grid_spec=grid_spec,
        input_output_aliases={6: 0},
        compiler_params=_cparams(("parallel", "arbitrary")),
        name="prompt_attn",
    )(qi_tbl, kj_tbl, q, kv_exp, kpe, kv_exp, mix)


def _decode_attn_kernel(pt_ref, qa_ref, qp_ref, cn_ref, kn_ref, kv_hbm, pe_hbm, o_ref,
                        kvbuf, pebuf, sem, m_ref, l_ref, acc_ref, *, layer, steps_per_seq):
    s = pl.program_id(0)
    last = pl.num_programs(0) - 1
    c = lax.rem(s, steps_per_seq)
    slot = lax.rem(s, 2)

    def page_copies(step, j, sl):
        page = pt_ref[step * PAGES_PER_STEP + j]
        return (pltpu.make_async_copy(kv_hbm.at[layer, page], kvbuf.at[sl, j], sem.at[0, sl]),
                pltpu.make_async_copy(pe_hbm.at[layer, page], pebuf.at[sl, j], sem.at[1, sl]))

    def wait_pages(sl):
        pltpu.make_async_copy(kv_hbm.at[layer, pl.ds(0, PAGES_PER_STEP)], kvbuf.at[sl], sem.at[0, sl]).wait()
        pltpu.make_async_copy(pe_hbm.at[layer, pl.ds(0, PAGES_PER_STEP)], pebuf.at[sl], sem.at[1, sl]).wait()

    @pl.when(s == 0)
    def _():
        for j in range(PAGES_PER_STEP):
            for cp in page_copies(0, j, 0):
                cp.start()

    nxt = jnp.minimum(s + 1, last)
    for j in range(PAGES_PER_STEP):
        for cp in page_copies(nxt, j, 1 - slot):
            cp.start()

    @pl.when(c == 0)
    def _():
        m_ref[...] = jnp.full_like(m_ref, -jnp.inf)
        l_ref[...] = jnp.zeros_like(l_ref)
        acc_ref[...] = jnp.zeros_like(acc_ref)

    qa = qa_ref[0]
    qp = qp_ref[0][:, :DH_ROPE]

    def update(sc, v):
        m_prev = m_ref[...]
        m_new = jnp.maximum(m_prev, jnp.max(sc, axis=1, keepdims=True))
        a = jnp.exp(m_prev - m_new)
        p = jnp.exp(sc - m_new)
        l_ref[...] = a * l_ref[...] + jnp.sum(p, axis=1, keepdims=True)
        acc_ref[...] = a * acc_ref[...] + _dot(p, v)
        m_ref[...] = m_new

    wait_pages(slot)
    kv = kvbuf[slot].reshape(PAGES_PER_STEP * PAGE_SIZE, KV_RANK)
    s_pe = jnp.concatenate([_dot(qp, pebuf[slot, j]) for j in range(PAGES_PER_STEP)], axis=1)
    update(_dot_nt(qa, kv) + s_pe, kv)

    @pl.when(c == steps_per_seq - 1)
    def _():
        cn = cn_ref[0]
        sc = _dot_nt(qa, cn) + _dot_nt(qp, kn_ref[0][:, :DH_ROPE])
        q_t = lax.broadcasted_iota(jnp.int32, sc.shape, 0) // B_HEADS
        k_t = lax.broadcasted_iota(jnp.int32, sc.shape, 1)
        update(jnp.where(k_t <= q_t, sc, -jnp.inf), cn)
        o_ref[0] = acc_ref[...] / l_ref[...]

    @pl.when(s == last)
    def _():
        wait_pages(1 - slot)


def _decode_attn(page_table, q_abs, q_pe, ckv_new, kpe_new, cache_kv, cache_pe_t, layer):
    nb, n_pages = page_table.shape
    steps_per_seq = n_pages // PAGES_PER_STEP
    rows = q_abs.shape[1]
    t_new = ckv_new.shape[1]
    per_b = lambda r, w: pl.BlockSpec((1, r, w), lambda s, pt: (s // steps_per_seq, 0, 0))
    grid_spec = pltpu.PrefetchScalarGridSpec(
        num_scalar_prefetch=1,
        grid=(nb * steps_per_seq,),
        in_specs=[per_b(rows, KV_RANK), per_b(rows, LANES), per_b(t_new, KV_RANK), per_b(t_new, LANES),
                  pl.BlockSpec(memory_space=pl.ANY), pl.BlockSpec(memory_space=pl.ANY)],
        out_specs=per_b(rows, KV_RANK),
        scratch_shapes=[pltpu.VMEM((2, PAGES_PER_STEP, PAGE_SIZE, KV_RANK), jnp.float32),
                        pltpu.VMEM((2, PAGES_PER_STEP, DH_ROPE, PAGE_SIZE), jnp.float32),
                        pltpu.SemaphoreType.DMA((2, 2)),
                        pltpu.VMEM((rows, 1), jnp.float32),
                        pltpu.VMEM((rows, 1), jnp.float32),
                        pltpu.VMEM((rows, KV_RANK), jnp.float32)],
    )
    return pl.pallas_call(
        partial(_decode_attn_kernel, layer=layer, steps_per_seq=steps_per_seq),
        out_shape=jax.ShapeDtypeStruct((nb, rows, KV_RANK), jnp.float32),
        grid_spec=grid_spec,
        compiler_params=_cparams(("arbitrary",)),
        name="decode_attn",
    )(page_table.reshape(-1), q_abs, q_pe, ckv_new, kpe_new, cache_kv, cache_pe_t)


def _pool_conv_tail(z_ext, d_ext, d_cur, cb, t0, n_rows, n_valid_ref, cw_ref, pw_ref, ps_ref, mix_ref):
    conv = z_ext[HALO - 2:HALO - 2 + n_rows] * cw_ref[0:1]
    for k in range(1, CONV_W):
        conv = conv + z_ext[HALO - 2 + k:HALO - 2 + k + n_rows] * cw_ref[k:k + 1]
    mix_ref[:, 0:C_WIDTH] = cb * conv
    sums = {1: d_ext}
    w = 1
    while w < MAX_WIN:
        prev = sums[w]
        sums[2 * w] = prev[w:] + prev[:-w]
        w *= 2
    t_idx = t0 + lax.broadcasted_iota(jnp.int32, (n_rows, 1), 0)
    for g, win in enumerate(POOL_WINDOWS):
        cols = slice(g * D_GROUP, (g + 1) * D_GROUP)
        s = sums[win]
        win_sum = s[s.shape[0] - n_rows:, cols]
        count = jnp.minimum(n_valid_ref + t_idx + 1, win).astype(jnp.float32)
        diff = win_sum / count - d_cur[:, cols]
        mix_ref[:, C_WIDTH + g * D_GROUP:C_WIDTH + (g + 1) * D_GROUP] = _dot(diff, pw_ref[g]) * ps_ref[:, cols]


def _odd_prompt_kernel(h_ref, halo_ref, ms_ref, cw_ref, pw_ref, ps_ref, mix_ref, z_ref, *, tiles_per_seq, n_tiles):
    i = pl.program_id(0)

    @pl.when(i < n_tiles)
    def _():
        first = (i % tiles_per_seq) == 0
        c1, c2, c3 = C_WIDTH, 2 * C_WIDTH, 3 * C_WIDTH
        z = h_ref[:, c1:c2] * h_ref[:, c2:c3]
        d = h_ref[:, c3:c3 + D_WIDTH]
        z_halo = jnp.where(first, 0.0, halo_ref[:, c1:c2] * halo_ref[:, c2:c3])
        d_halo = jnp.where(first, 0.0, halo_ref[:, c3:c3 + D_WIDTH])
        z_ext = jnp.concatenate([z_halo, z], axis=0)
        d_ext = jnp.concatenate([d_halo, d], axis=0)
        t0 = (i % tiles_per_seq) * ROW_TILE
        _pool_conv_tail(z_ext, d_ext, d, h_ref[:, 0:c1], t0, ROW_TILE, 0, cw_ref, pw_ref, ps_ref, mix_ref)
        z_ref[0] = z[ROW_TILE - 8:]

    @pl.when(i >= n_tiles)
    def _():
        mix_ref[...] = ms_ref[...]


def _odd_prompt(h, n, mix_sample, conv_w, pool_w, pool_scale, seq):
    tm = ROW_TILE
    tiles_per_seq = seq // tm
    n_tiles = n // tm
    full = lambda a: pl.BlockSpec(a.shape, lambda i: (0,) * a.ndim)
    last = lambda i: jnp.minimum(i, n_tiles - 1)
    return pl.pallas_call(
        partial(_odd_prompt_kernel, tiles_per_seq=tiles_per_seq, n_tiles=n_tiles),
        out_shape=(jax.ShapeDtypeStruct((h.shape[0], C_WIDTH + D_WIDTH), jnp.float32),
                   jax.ShapeDtypeStruct((n_tiles, 8, C_WIDTH), jnp.float32)),
        grid=(h.shape[0] // tm,),
        in_specs=[pl.BlockSpec((tm, h.shape[1]), lambda i: (last(i), 0)),
                  pl.BlockSpec((HALO, h.shape[1]), lambda i: (jnp.maximum(last(i) * (tm // HALO) - 1, 0), 0)),
                  pl.BlockSpec((tm, C_WIDTH + D_WIDTH), lambda i: (jnp.maximum(i - n_tiles, 0), 0)),
                  full(conv_w), full(pool_w), full(pool_scale)],
        out_specs=(pl.BlockSpec((tm, C_WIDTH + D_WIDTH), lambda i: (i, 0)),
                   pl.BlockSpec((1, 8, C_WIDTH), lambda i: (last(i), 0, 0))),
        compiler_params=_cparams(("arbitrary",)),
        name="odd_prompt",
    )(h, h, mix_sample, conv_w, pool_w, pool_scale)


def _odd_sample_kernel(h_ref, sc_ref, sp_ref, cw_ref, pw_ref, ps_ref, mix_ref, z_ref, *, n_valid, t_new):
    c1, c2, c3 = C_WIDTH, 2 * C_WIDTH, 3 * C_WIDTH
    cw = cw_ref[...]
    z = [h_ref[t][:, c1:c2] * h_ref[t][:, c2:c3] for t in range(t_new)]
    zp = [sc_ref[k] for k in range(CONV_W - 1)] + z
    pp = [sp_ref[k] for k in range(MAX_WIN - 1)] + [h_ref[t][:, c3:c3 + D_WIDTH] for t in range(t_new)]
    for t in range(t_new):
        conv = zp[t] * cw[0:1]
        for k in range(1, CONV_W):
            conv = conv + zp[t + k] * cw[k:k + 1]
        mix_ref[t, :, 0:C_WIDTH] = h_ref[t][:, 0:c1] * conv
        z_ref[t] = z[t]
        cur = MAX_WIN - 1 + t
        for g, win in enumerate(POOL_WINDOWS):
            cols = slice(g * D_GROUP, (g + 1) * D_GROUP)
            win_sum = pp[cur][:, cols]
            for k in range(1, win):
                win_sum = win_sum + pp[cur - k][:, cols]
            count = float(min(n_valid + t + 1, win))
            diff = win_sum / count - pp[cur][:, cols]
            mix_ref[t, :, C_WIDTH + g * D_GROUP:C_WIDTH + (g + 1) * D_GROUP] = (
                _dot(diff, pw_ref[g]) * ps_ref[:, cols])


def _odd_sample(h_tm, conv_state_tm, pool_state_tm, conv_w, pool_w, pool_scale, n_valid):
    t_new, nb, width = h_tm.shape
    bb = 32
    full = lambda a: pl.BlockSpec(a.shape, lambda i: (0,) * a.ndim)
    tm_spec = lambda t, w: pl.BlockSpec((t, bb, w), lambda i: (0, i, 0))
    return pl.pallas_call(
        partial(_odd_sample_kernel, n_valid=n_valid, t_new=t_new),
        out_shape=(jax.ShapeDtypeStruct((t_new, nb, C_WIDTH + D_WIDTH), jnp.float32),
                   jax.ShapeDtypeStruct((t_new, nb, C_WIDTH), jnp.float32)),
        grid=(nb // bb,),
        in_specs=[tm_spec(t_new, width), tm_spec(CONV_W - 1, C_WIDTH), tm_spec(MAX_WIN - 1, D_WIDTH),
                  full(conv_w), full(pool_w), full(pool_scale)],
        out_specs=(tm_spec(t_new, C_WIDTH + D_WIDTH), tm_spec(t_new, C_WIDTH)),
        compiler_params=_cparams(("parallel",)),
        name="odd_sample",
    )(h_tm, conv_state_tm, pool_state_tm, conv_w, pool_w, pool_scale)


def _router_kernel(x_ref, w_ref, b_ref, e_ref, g_ref):
    logits = jnp.dot(x_ref[...], w_ref[...], preferred_element_type=jnp.float32,
                     precision=lax.Precision.HIGHEST) + b_ref[...]
    lane = lax.broadcasted_iota(jnp.int32, logits.shape, 1).astype(jnp.float32)
    big = float(LANES)
    neg = -jnp.inf
    is_c = lane < N_GROUPS
    lc = jnp.where(is_c, logits, neg)
    ec = jnp.exp(lc - jnp.max(lc, axis=1, keepdims=True))
    pc = ec / jnp.sum(ec, axis=1, keepdims=True)
    p_g = jnp.max(pc, axis=1, keepdims=True)
    g_sel = jnp.min(jnp.where(is_c & (pc == p_g), lane, big), axis=1, keepdims=True)
    lo = N_GROUPS + g_sel * EXPERTS_PER_GROUP
    in_g = (lane >= lo) & (lane < lo + EXPERTS_PER_GROUP)
    lf = jnp.where(in_g, logits, neg)
    ef = jnp.exp(lf - jnp.max(lf, axis=1, keepdims=True))
    pf = ef / jnp.sum(ef, axis=1, keepdims=True)
    p1 = jnp.max(pf, axis=1, keepdims=True)
    j1 = jnp.min(jnp.where(in_g & (pf == p1), lane, big), axis=1, keepdims=True)
    rest = in_g & (lane != j1)
    pf2 = jnp.where(rest, pf, -1.0)
    p2 = jnp.max(pf2, axis=1, keepdims=True)
    j2 = jnp.min(jnp.where(rest & (pf2 == p2), lane, big), axis=1, keepdims=True)
    tot = p1 + p2
    e_ref[...] = jnp.where(lane == 0, j1 - N_GROUPS, jnp.where(lane == 1, j2 - N_GROUPS, 0.0)).astype(jnp.int32)
    g_ref[...] = jnp.where(lane == 0, p_g * p1 / tot, jnp.where(lane == 1, p_g * p2 / tot, 0.0))


def _router(x, w, b):
    n, d = x.shape
    tm = MM_TM
    return pl.pallas_call(
        _router_kernel,
        out_shape=(jax.ShapeDtypeStruct((n, LANES), jnp.int32),
                   jax.ShapeDtypeStruct((n, LANES), jnp.float32)),
        grid=(n // tm,),
        in_specs=[pl.BlockSpec((tm, d), lambda i: (i, 0)),
                  pl.BlockSpec((d, LANES), lambda i: (0, 0)),
                  pl.BlockSpec((1, LANES), lambda i: (0, 0))],
        out_specs=(pl.BlockSpec((tm, LANES), lambda i: (i, 0)),
                   pl.BlockSpec((tm, LANES), lambda i: (i, 0))),
        compiler_params=_cparams(("parallel",)),
        name="router",
    )(x, w, b)


def _expert_kernel(be_ref, nu_ref, tok_ref, dst_ref, x_hbm, wg_ref, wu_ref, wd_ref, y_hbm,
                   xbuf, obuf, gsem, ssem):
    b = pl.program_id(0)
    n_used = nu_ref[0]
    slot = lax.rem(b, 2)

    def gather_row(blk, r, s):
        tok = tok_ref[blk * MOE_BM + r]
        return pltpu.make_async_copy(x_hbm.at[pl.ds(tok, 1)], xbuf.at[s, pl.ds(r, 1)], gsem.at[s])

    def scatter_row(base, r, s):
        dst = dst_ref[base + r]
        return pltpu.make_async_copy(obuf.at[s, pl.ds(r, 1)], y_hbm.at[pl.ds(dst, 1)], ssem.at[s])

    def wait_gather(s):
        pltpu.make_async_copy(x_hbm.at[pl.ds(0, MOE_BM)], xbuf.at[s], gsem.at[s]).wait()

    def wait_scatter(s):
        pltpu.make_async_copy(obuf.at[s], y_hbm.at[pl.ds(0, MOE_BM)], ssem.at[s]).wait()

    @pl.when(b == 0)
    def _():
        obuf[1] = jnp.zeros(obuf.shape[1:], obuf.dtype)
        fill = pltpu.make_async_copy(obuf.at[1], y_hbm.at[pl.ds(y_hbm.shape[0] - 2 * MOE_BM, MOE_BM)], ssem.at[0])
        fill.start()
        fill.wait()

        def body(r, c):
            gather_row(0, r, 0).start()
            return c
        lax.fori_loop(0, MOE_BM, body, 0)

    @pl.when(b < n_used)
    def _():
        wait_gather(slot)
        nxt = jnp.minimum(b + 1, n_used - 1)
        for r in range(MOE_BM):
            gather_row(nxt, r, 1 - slot).start()
        for r in range(MOE_BM):
            scatter_row(b * MOE_BM, r, 1 - slot).start()
        x = xbuf[slot]
        hid = jax.nn.silu(_dot(x, wg_ref[0])) * _dot(x, wu_ref[0])
        out = _dot(hid, wd_ref[0])

        @pl.when(b >= 1)
        def _():
            wait_scatter(slot)

        obuf[slot] = out

        @pl.when(b == n_used - 1)
        def _():
            def body(r, c):
                scatter_row((b + 1) * MOE_BM, r, slot).start()
                return c
            lax.fori_loop(0, MOE_BM, body, 0)
            wait_scatter(slot)
            wait_scatter(1 - slot)
            wait_gather(1 - slot)


def _expert_ffn(block_e, n_used, row_tok, dst_tbl, x, w_gate, w_up, w_down, layer, y_rows):
    n, d = x.shape
    n_blocks = block_e.shape[0]
    wspec = lambda s: pl.BlockSpec((None,) + s, lambda i, be, nu, tok, dst: (layer, be[i], 0, 0))
    grid_spec = pltpu.PrefetchScalarGridSpec(
        num_scalar_prefetch=4,
        grid=(n_blocks,),
        in_specs=[pl.BlockSpec(memory_space=pl.ANY),
                  wspec((1, d, D_EXPERT)), wspec((1, d, D_EXPERT)), wspec((1, D_EXPERT, d))],
        out_specs=pl.BlockSpec(memory_space=pl.ANY),
        scratch_shapes=[pltpu.VMEM((2, MOE_BM, d), jnp.float32),
                        pltpu.VMEM((2, MOE_BM, d), jnp.float32),
                        pltpu.SemaphoreType.DMA((2,)),
                        pltpu.SemaphoreType.DMA((2,))],
    )
    return pl.pallas_call(
        _expert_kernel,
        out_shape=jax.ShapeDtypeStruct((y_rows, d), jnp.float32),
        grid_spec=grid_spec,
        compiler_params=_cparams(("arbitrary",)),
        name="expert_ffn",
    )(block_e, n_used, row_tok, dst_tbl, x, w_gate, w_up, w_down)


def _combine_ln_kernel(x_ref, y0_ref, y1_ref, gate_ref, g_ref, b_ref, o_ref):
    gate = gate_ref[...]
    moe = gate[:, 0:1] * y0_ref[...] + gate[:, 1:2] * y1_ref[...]
    o_ref[...] = _ln(ALPHA * x_ref[...] + moe, g_ref[...], b_ref[...])


def _combine_ln(x, y, gates, g, b):
    n, d = x.shape
    tm = MM_TM
    row = lambda w: pl.BlockSpec((tm, w), lambda i: (i, 0))
    vec = pl.BlockSpec((1, d), lambda i: (0, 0))
    return pl.pallas_call(
        _combine_ln_kernel,
        out_shape=jax.ShapeDtypeStruct((n, d), jnp.float32),
        grid=(n // tm,),
        in_specs=[row(d), row(d), pl.BlockSpec((tm, d), lambda i: (n // tm + i, 0)), row(LANES), vec, vec],
        out_specs=row(d),
        compiler_params=_cparams(("parallel",)),
        name="combine_ln",
    )(x, y, y, gates, g.reshape(1, d), b.reshape(1, d))


def _moe_ln(x, rc_w, rc_b, rf_w, rf_b, w_gate, w_up, w_down, layer, ln_g, ln_b):
    n, d = x.shape
    pad = LANES - N_GROUPS - N_EXPERTS
    rw = jnp.concatenate([rc_w, rf_w, jnp.zeros((d, pad), jnp.float32)], axis=1)
    rb = jnp.concatenate([rc_b, rf_b, jnp.zeros((pad,), jnp.float32)]).reshape(1, LANES)
    e_pad, gates = _router(x, rw, rb)
    a = n * TOP_K
    flat_e = e_pad[:, :TOP_K].reshape(-1)
    order = jnp.argsort(flat_e).astype(jnp.int32)
    counts = jnp.sum(flat_e[:, None] == jnp.arange(N_EXPERTS, dtype=jnp.int32)[None, :], axis=0, dtype=jnp.int32)
    padded = (counts + MOE_BM - 1) // MOE_BM * MOE_BM
    pad_end = jnp.cumsum(padded)
    pad_start = pad_end - padded
    start = jnp.cumsum(counts) - counts
    n_blocks = a // MOE_BM + N_EXPERTS
    rows = n_blocks * MOE_BM
    blk_first = jnp.arange(n_blocks, dtype=jnp.int32) * MOE_BM
    block_e = jnp.minimum(jnp.sum(blk_first[:, None] >= pad_end[None, :], axis=1, dtype=jnp.int32), N_EXPERTS - 1)
    n_used = (pad_end[-1:] // MOE_BM).astype(jnp.int32)
    r = jnp.arange(rows, dtype=jnp.int32)
    e_r = jnp.repeat(block_e, MOE_BM)
    valid = (r < pad_start[e_r] + counts[e_r]) & (r < pad_end[-1])
    flat_r = order[jnp.clip(r - pad_start[e_r] + start[e_r], 0, a - 1)]
    tok_r, slot_r = flat_r // TOP_K, flat_r % TOP_K
    row_tok = jnp.where(valid, tok_r, 0)
    spare = TOP_K * n
    row_dst = jnp.where(valid, slot_r * n + tok_r, spare + ((r // MOE_BM) % 2) * MOE_BM + r % MOE_BM)
    dst_tbl = jnp.concatenate([spare + MOE_BM + jnp.arange(MOE_BM, dtype=jnp.int32), row_dst])
    y = _expert_ffn(block_e, n_used, row_tok, dst_tbl, x, w_gate, w_up, w_down, layer,
                    spare + 2 * MOE_BM)
    return _combine_ln(x, y, gates, ln_g, ln_b)


def _rope_tables(pos):
    inv = ROPE_THETA ** (-jnp.arange(0, DH_ROPE, 2, dtype=jnp.float32) / DH_ROPE)
    ang = pos[:, None] * inv[None, :]
    cos, sin = jnp.cos(ang), jnp.sin(ang)
    z = jnp.zeros_like(cos)
    c = jnp.concatenate([cos, cos, z, z], axis=1)
    s1 = jnp.concatenate([-sin, z, z, z], axis=1)
    s2 = jnp.concatenate([z, sin, z, z], axis=1)
    return c, s1, s2


def _even_layer(x, n_prompt, bp, seq, bs, t_new, rope, page_table, cache_kv, cache_pe, layer,
                w_in, sgu_ln_g, sgu_ln_b, sgu_w, sgu_b, q_norm_g, w_q_up, kv_norm_g, w_uk, w_uv, w_out,
                ln_g, ln_b):
    w_in_p = jnp.pad(w_in, ((0, 0), (0, IN_EVEN_PAD - w_in.shape[1])))
    w_q2 = jnp.pad(w_q_up, ((0, 0), (0, 0), (0, HEAD_PAD - DH_NOPE - DH_ROPE))).reshape(Q_RANK, B_HEADS * HEAD_PAD)
    reps = CHUNK // t_new
    w_small = jnp.tril(sgu_w[:, :t_new, :t_new])
    eye = jnp.eye(reps, dtype=jnp.float32)
    wm_sample = jnp.einsum('ab,hij->haibj', eye, w_small).reshape(A_HEADS, CHUNK, CHUNK)
    wm = jnp.stack([jnp.tril(sgu_w), wm_sample])
    bias_p = jnp.repeat(sgu_b.T, HEAD_DIM, axis=1)
    bias_s = jnp.tile(jnp.repeat(sgu_b[:, :t_new].T, HEAD_DIM, axis=1), (reps, 1))
    bias = jnp.stack([bias_p, bias_s])
    w_kv_exp = jnp.concatenate([w_uk.reshape(KV_RANK, -1), w_uv.reshape(KV_RANK, -1)], axis=1)
    w_uk_t = jnp.transpose(w_uk, (1, 2, 0))
    w_uv_h = jnp.transpose(w_uv, (1, 0, 2))

    h = _matmul(x, w_in_p, tn=IN_EVEN_PAD // 3, name="in_even")
    mix, vn, q, ckv, kpe = _even_post(
        h, wm, bias, sgu_ln_g.reshape(1, -1), sgu_ln_b.reshape(1, -1), q_norm_g.reshape(1, -1), w_q2,
        kv_norm_g.reshape(1, -1), *rope, n_prompt // ROW_TILE)

    kv_exp = _matmul(ckv, w_kv_exp, rows=n_prompt, name="kv_expand")
    mix = _prompt_attn(q, kv_exp, kpe, mix, bp, seq)

    q_s = q[n_prompt:]
    q_abs = _head_mm(q_s, w_uk_t, x_block=DH_NOPE, x_stride=HEAD_PAD // DH_NOPE, name="q_absorb")
    q_abs = q_abs.reshape(bs, t_new * B_HEADS, KV_RANK)
    q_pe = q_s.reshape(bs * t_new, B_HEADS, HEAD_PAD)[:, :, LANES:].reshape(bs, t_new * B_HEADS, LANES)
    pad_new = ((0, 0), (0, 8 - t_new), (0, 0))
    o_lat = _decode_attn(page_table, q_abs, q_pe, jnp.pad(ckv[n_prompt:].reshape(bs, t_new, KV_RANK), pad_new),
                         jnp.pad(kpe[n_prompt:].reshape(bs, t_new, LANES), pad_new), cache_kv, cache_pe, layer)
    o_s = _head_mm(o_lat.reshape(bs * t_new, B_HEADS * KV_RANK), w_uv_h, x_block=KV_RANK, x_stride=1,
                   name="o_up")

    mix = lax.dynamic_update_slice(mix, o_s, (n_prompt, A_WIDTH))
    x_new = _matmul(mix, w_out, resid=x, g=ln_g, b=ln_b, name="out_even")
    return x_new, ckv, kpe[:, :DH_ROPE], vn


def _odd_layer(x, n_prompt, bp, seq, bs, t_new, n_valid, conv_state, pool_state,
               w_in, conv_w, pool_w, pool_scale, w_out, ln_g, ln_b):
    h = _matmul(x, w_in, tn=C_WIDTH, name="in_odd")
    ps = pool_scale.reshape(1, -1)
    h_s = h[n_prompt:].reshape(bs, t_new, -1)
    mix_s, z_s = _odd_sample(jnp.swapaxes(h_s, 0, 1), jnp.swapaxes(conv_state, 0, 1),
                             jnp.swapaxes(pool_state, 0, 1), conv_w, pool_w, ps, n_valid)
    mix, z_tail = _odd_prompt(h, n_prompt, jnp.swapaxes(mix_s, 0, 1).reshape(bs * t_new, -1),
                              conv_w, pool_w, ps, seq)
    x_new = _matmul(mix, w_out, resid=x, g=ln_g, b=ln_b, name="out_odd")
    tiles_per_seq = seq // ROW_TILE
    conv_p = z_tail.reshape(bp, tiles_per_seq, 8, C_WIDTH)[:, -1, 8 - (CONV_W - 1):]
    d_p = h[:n_prompt, 3 * C_WIDTH:].reshape(bp, seq, D_WIDTH)
    pool_p = d_p[:, seq - (MAX_WIN - 1):]
    zp_s = jnp.concatenate([conv_state, jnp.swapaxes(z_s, 0, 1)], axis=1)
    conv_s = zp_s[:, -(CONV_W - 1):]
    pp_s = jnp.concatenate([pool_state, h_s[:, :, 3 * C_WIDTH:]], axis=1)
    pool_s = pp_s[:, -(MAX_WIN - 1):]
    return x_new, conv_p, pool_p, conv_s, pool_s


def kernel(x_prompt, x_sample, cache_kv_latent, cache_k_rope, state_conv, state_pool, page_table,
           w_in_even, sgu_ln_g, sgu_ln_b, sgu_w, sgu_b, q_norm_g, w_q_up, kv_norm_g, w_uk, w_uv, w_out_even,
           w_in_odd, conv_w, pool_w, pool_scale, w_out_odd,
           ln_mix_g, ln_mix_b, ln_ffn_g, ln_ffn_b,
           router_coarse_w, router_coarse_b, router_fine_w, router_fine_b,
           w_exp_gate, w_exp_up, w_exp_down):
    bp, seq, d = x_prompt.shape
    bs, t_new, _ = x_sample.shape
    n_prompt = bp * seq
    past_len = page_table.shape[1] * PAGE_SIZE
    x = jnp.concatenate([x_prompt.reshape(n_prompt, d), x_sample.reshape(bs * t_new, d)], axis=0)
    pos = jnp.concatenate([jnp.tile(jnp.arange(seq, dtype=jnp.float32), bp),
                           jnp.tile(past_len + jnp.arange(t_new, dtype=jnp.float32), bs)])
    rope = _rope_tables(pos)
    cache_pe_t = jnp.swapaxes(cache_k_rope, 2, 3)

    lat_p, pe_p, conv_p, pool_p = [], [], [], []
    lat_s, pe_s, v_s, conv_s, pool_s = [], [], [], [], []
    for l in range(DEPTH):
        i = l // 2
        if l % 2 == 0:
            x, ckv, kpe, vn = _even_layer(
                x, n_prompt, bp, seq, bs, t_new, rope, page_table, cache_kv_latent, cache_pe_t, i,
                w_in_even[i], sgu_ln_g[i], sgu_ln_b[i], sgu_w[i], sgu_b[i], q_norm_g[i], w_q_up[i],
                kv_norm_g[i], w_uk[i], w_uv[i], w_out_even[i], ln_mix_g[l], ln_mix_b[l])
            lat_p.append(ckv[:n_prompt].reshape(bp, seq, KV_RANK))
            pe_p.append(kpe[:n_prompt].reshape(bp, seq, DH_ROPE))
            lat_s.append(ckv[n_prompt:].reshape(bs, t_new, KV_RANK))
            pe_s.append(kpe[n_prompt:].reshape(bs, t_new, DH_ROPE))
            v_s.append(vn[n_prompt:].reshape(bs, t_new, A_WIDTH))
        else:
            x, cp, pp, cs, ps = _odd_layer(
                x, n_prompt, bp, seq, bs, t_new, past_len, state_conv[i], state_pool[i],
                w_in_odd[i], conv_w[i], pool_w[i], pool_scale[i], w_out_odd[i], ln_mix_g[l], ln_mix_b[l])
            conv_p.append(cp)
            pool_p.append(pp)
            conv_s.append(cs)
            pool_s.append(ps)
        x = _moe_ln(x, router_coarse_w[l], router_coarse_b[l], router_fine_w[l], router_fine_b[l],
                    w_exp_gate, w_exp_up, w_exp_down, l, ln_ffn_g[l], ln_ffn_b[l])
    return (x[:n_prompt].reshape(bp, seq, d), x[n_prompt:].reshape(bs, t_new, d),
            jnp.stack(lat_p), jnp.stack(pe_p), jnp.stack(conv_p), jnp.stack(pool_p),
            jnp.stack(lat_s), jnp.stack(pe_s), jnp.stack(v_s), jnp.stack(conv_s), jnp.stack(pool_s))
```
